```python
import math
import jax, jax.numpy as jnp
from jax import lax
import numpy as np


D_MODEL = 1024
BATCH = 2
SEQ = 8192
DEPTH = 4
DEC_BATCH = 1
DEC_SEQ = 16384
PAST_LEN = 128

HEAD_DIM = 64
A_HEADS = 8
A_KV_HEADS = 2
A_GROUP = A_HEADS // A_KV_HEADS
B_HEADS = 8
MIX_WIDTH = (A_HEADS + B_HEADS) * HEAD_DIM
WINDOW = 128
BLK = 128
GRID_W = 64
NA_ROWS = 8
NA_COLS = 16
D_FF = 2816
PLE_DIM = 256
EPS = 1e-6

QA_W = A_HEADS * HEAD_DIM
KVA_W = A_KV_HEADS * HEAD_DIM
QB_W = B_HEADS * HEAD_DIM
IN_WIDTH = QA_W + 2 * KVA_W + 3 * QB_W
IN_SPLITS = [QA_W, QA_W + KVA_W, QA_W + 2 * KVA_W, QA_W + 2 * KVA_W + QB_W, QA_W + 2 * KVA_W + 2 * QB_W]

N_FFN1_PRE, N_FFN1_POST, N_MIX_PRE, N_MIX_POST, N_FFN2_PRE, N_FFN2_POST, N_PLE_PRE, N_PLE_POST, N_GROUP = range(9)
N_NORMS = 9

kernel_name = 'hymba_parallel_window_gqa_natten_macaron_encoder'


def rmsnorm(x, g):
    x32 = x.astype(jnp.float32)
    y = x32 * lax.rsqrt(jnp.mean(x32 * x32, axis=-1, keepdims=True) + EPS)
    return (y * g.astype(jnp.float32)).astype(x.dtype)


def swiglu(x, wg, wu, wd):
    return (jax.nn.silu(x @ wg) * (x @ wu)) @ wd


def alibi_slopes(n):
    return jnp.exp2(-(8.0 / n) * jnp.arange(1, n + 1, dtype=jnp.float32))


def window_attention(q, k, v, sink):
    B, L = q.shape[0], q.shape[1]
    nb = L // BLK
    qb = q.reshape(B, nb, BLK, A_KV_HEADS, A_GROUP, HEAD_DIM)

    def bands(t):
        tp = jnp.pad(t.reshape(B, L, A_KV_HEADS, HEAD_DIM), ((0, 0), (BLK, BLK), (0, 0), (0, 0)))
        tp = tp.reshape(B, nb + 2, BLK, A_KV_HEADS, HEAD_DIM)
        return jnp.concatenate([tp[:, :nb], tp[:, 1:nb + 1], tp[:, 2:]], axis=2)

    kb, vb = bands(k), bands(v)
    s = jnp.einsum('bnqkgd,bnskd->bnkgqs', qb, kb).astype(jnp.float32) * (HEAD_DIM ** -0.5)
    i = jnp.arange(BLK)
    j = jnp.arange(3 * BLK)
    rel = j[None, :] - BLK - i[:, None]
    dist = jnp.abs(rel).astype(jnp.float32)
    kpos = (jnp.arange(nb)[:, None] - 1) * BLK + j[None, :]
    mask = (jnp.abs(rel) <= WINDOW)[None] & ((kpos >= 0) & (kpos < L))[:, None, :]
    slopes = alibi_slopes(A_HEADS).reshape(A_KV_HEADS, A_GROUP)
    s = s - slopes[:, :, None, None] * dist
    s = jnp.where(mask[None, :, None, None], s, -jnp.inf)
    sk = sink.astype(jnp.float32).reshape(1, 1, A_KV_HEADS, A_GROUP, 1, 1)
    m = jnp.maximum(jnp.max(s, axis=-1, keepdims=True), sk)
    e = jnp.exp(s - m)
    pr = e / (jnp.sum(e, axis=-1, keepdims=True) + jnp.exp(sk - m))
    o = jnp.einsum('bnkgqs,bnskd->bnqkgd', pr.astype(v.dtype), vb)
    return o.reshape(B, L, QA_W)


def neighborhood_attention(q, k, v, rpb):
    B, L = q.shape[0], q.shape[1]
    rows = L // GRID_W
    kr = min(NA_ROWS, rows)
    qg = q.reshape(B, rows, GRID_W, B_HEADS, HEAD_DIM)
    kg = k.reshape(B, rows, GRID_W, B_HEADS, HEAD_DIM)
    vg = v.reshape(B, rows, GRID_W, B_HEADS, HEAD_DIM)
    r = jnp.arange(rows)
    row_idx = jnp.clip(r - kr // 2, 0, rows - kr)[:, None] + jnp.arange(kr)[None, :]
    kw = kg[:, row_idx]
    vw = vg[:, row_idx]
    s = jnp.einsum('brqhd,brjkhd->brhqjk', qg, kw).astype(jnp.float32) * (HEAD_DIM ** -0.5)
    c = jnp.arange(GRID_W)
    cs = jnp.clip(c - NA_COLS // 2, 0, GRID_W - NA_COLS)
    col_ok = (c[None, :] >= cs[:, None]) & (c[None, :] < cs[:, None] + NA_COLS)
    dr = row_idx - r[:, None] + (NA_ROWS - 1)
    dc = jnp.clip(c[None, :] - c[:, None], -(NA_COLS - 1), NA_COLS - 1) + (NA_COLS - 1)
    bias = rpb.astype(jnp.float32)[:, dr][:, :, :, dc]
    bias = bias.transpose(1, 0, 3, 2, 4)
    s = s + bias[None]
    s = jnp.where(col_ok[:, None, :], s, -jnp.inf)
    pr = jax.nn.softmax(s.reshape(B, rows, B_HEADS, GRID_W, kr * GRID_W), axis=-1).reshape(s.shape)
    o = jnp.einsum('brhqjk,brjkhd->brqhd', pr.astype(v.dtype), vw)
    return o.reshape(B, L, QB_W)


def mixing(u, w_in, w_out, sink, rpb, g_group):
    z = u @ w_in
    qa, ka, va, qb, kb, vb = jnp.split(z, IN_SPLITS, axis=-1)
    oa = rmsnorm(window_attention(qa, ka, va, sink), g_group[:QA_W])
    ob = rmsnorm(neighborhood_attention(qb, kb, vb, rpb), g_group[QA_W:])
    return jnp.concatenate([oa, ob], axis=-1) @ w_out


def trunk(x, p, w_ffn1_gate, w_ffn1_up, w_ffn1_down, w_in, w_out, sink, rpb,
          w_ffn2_gate, w_ffn2_up, w_ffn2_down, w_ple_proj, w_ple_gate, norm_g):
    h = x
    for i in range(DEPTH):
        g = norm_g[i]
        f1 = swiglu(rmsnorm(h, g[N_FFN1_PRE]), w_ffn1_gate[i], w_ffn1_up[i], w_ffn1_down[i])
        h = h + 0.5 * rmsnorm(f1, g[N_FFN1_POST])
        mix = mixing(rmsnorm(h, g[N_MIX_PRE]), w_in[i], w_out[i], sink[i], rpb[i], g[N_GROUP])
        h = h + rmsnorm(mix, g[N_MIX_POST])
        f2 = swiglu(rmsnorm(h, g[N_FFN2_PRE]), w_ffn2_gate[i], w_ffn2_up[i], w_ffn2_down[i])
        h = h + 0.5 * rmsnorm(f2, g[N_FFN2_POST])
        gate = jax.nn.sigmoid(rmsnorm(h, g[N_PLE_PRE]) @ w_ple_gate[i])
        h = h + rmsnorm((p[i] @ w_ple_proj[i]) * gate, g[N_PLE_POST])
    return h


def setup_inputs(seed: int = 0) -> dict:
    key = jax.random.key(seed)
    ks = jax.random.split(key, 20)
    f32 = jnp.float32

    def w(k, shape, fan_in):
        return jax.random.normal(k, shape, f32) * (fan_in ** -0.5)

    return {
        'x_prompt': jax.random.normal(ks[0], (BATCH, SEQ, D_MODEL), f32),
        'x_sample': jax.random.normal(ks[1], (DEC_BATCH, DEC_SEQ, D_MODEL), f32),
        'p_prompt': jax.random.normal(ks[2], (DEPTH, BATCH, SEQ, PLE_DIM), f32),
        'p_sample': jax.random.normal(ks[3], (DEPTH, DEC_BATCH, DEC_SEQ, PLE_DIM), f32),
        'w_ffn1_gate': w(ks[4], (DEPTH, D_MODEL, D_FF), D_MODEL),
        'w_ffn1_up': w(ks[5], (DEPTH, D_MODEL, D_FF), D_MODEL),
        'w_ffn1_down': w(ks[6], (DEPTH, D_FF, D_MODEL), D_FF),
        'w_in': w(ks[7], (DEPTH, D_MODEL, IN_WIDTH), D_MODEL),
        'w_out': w(ks[8], (DEPTH, MIX_WIDTH, D_MODEL), MIX_WIDTH),
        'sink': jax.random.normal(ks[9], (DEPTH, A_HEADS), f32) * 0.5,
        'rpb': jax.random.normal(ks[10], (DEPTH, B_HEADS, 2 * NA_ROWS - 1, 2 * NA_COLS - 1), f32) * 0.1,
        'w_ffn2_gate': w(ks[11], (DEPTH, D_MODEL, D_FF), D_MODEL),
        'w_ffn2_up': w(ks[12], (DEPTH, D_MODEL, D_FF), D_MODEL),
        'w_ffn2_down': w(ks[13], (DEPTH, D_FF, D_MODEL), D_FF),
        'w_ple_proj': w(ks[14], (DEPTH, PLE_DIM, D_MODEL), PLE_DIM),
        'w_ple_gate': w(ks[15], (DEPTH, D_MODEL, D_MODEL), D_MODEL),
        'norm_g': 1.0 + 0.01 * jax.random.normal(ks[16], (DEPTH, N_NORMS, D_MODEL), f32),
    }


def reference(x_prompt, x_sample, p_prompt, p_sample, w_ffn1_gate, w_ffn1_up, w_ffn1_down,
              w_in, w_out, sink, rpb, w_ffn2_gate, w_ffn2_up, w_ffn2_down,
              w_ple_proj, w_ple_gate, norm_g):
    y_prompt = trunk(x_prompt, p_prompt, w_ffn1_gate, w_ffn1_up, w_ffn1_down, w_in, w_out, sink, rpb,
                     w_ffn2_gate, w_ffn2_up, w_ffn2_down, w_ple_proj, w_ple_gate, norm_g)
    y_sample = trunk(x_sample, p_sample, w_ffn1_gate, w_ffn1_up, w_ffn1_down, w_in, w_out, sink, rpb,
                     w_ffn2_gate, w_ffn2_up, w_ffn2_down, w_ple_proj, w_ple_gate, norm_g)
    return (y_prompt, y_sample)
```

```python
import functools

import jax
import jax.numpy as jnp
from jax import lax
from jax.experimental import pallas as pl
from jax.experimental.pallas import tpu as pltpu

D_MODEL = 1024
DEPTH = 4
HEAD_DIM = 64
A_HEADS = 8
A_KV_HEADS = 2
A_GROUP = A_HEADS // A_KV_HEADS
B_HEADS = 8
WINDOW = 128
BLK = 128
GRID_W = 64
NA_ROWS = 8
NA_COLS = 16
D_FF = 2816
PLE_DIM = 256
EPS = 1e-6
QA_W = A_HEADS * HEAD_DIM
KVA_W = A_KV_HEADS * HEAD_DIM
QB_W = B_HEADS * HEAD_DIM
IN_WIDTH = QA_W + 2 * KVA_W + 3 * QB_W
(N_FFN1_PRE, N_FFN1_POST, N_MIX_PRE, N_MIX_POST, N_FFN2_PRE, N_FFN2_POST,
 N_PLE_PRE, N_PLE_POST, N_GROUP) = range(9)
N_NORMS = 9

LANES = 128
MXU_COLS = 256
VMEM_LIMIT_BYTES = 56 * 1024 * 1024

FF_CHUNK = MXU_COLS
N_FF_CHUNKS = D_FF // FF_CHUNK
TOKEN_TILE = 512
ATT_TILE = 512
WA_BLOCKS = ATT_TILE // BLK
NA_TILE_ROWS = ATT_TILE // GRID_W
WA_HALO = BLK
NA_HALO = (NA_ROWS // 2) * GRID_W
BAND = 3 * BLK
NA_KEYS = NA_ROWS * GRID_W
N_PAIRS = B_HEADS // 2

assert D_FF % FF_CHUNK == 0 and ATT_TILE % BLK == 0 and ATT_TILE % GRID_W == 0
assert 2 * HEAD_DIM == LANES and NA_TILE_ROWS == NA_ROWS

F32 = jnp.float32
BF16 = jnp.bfloat16
NT_DIMS = (((1,), (1,)), ((), ()))


def _rms(x, g):
    ms = jnp.mean(x * x, axis=-1, keepdims=True)
    return x * lax.rsqrt(ms + EPS) * g


def _row(g_ref, k):
    return g_ref[k:k + 1, :]


def _swiglu(h, g_pre, wg_ref, wu_ref, wd_ref, xn_scr, acc_scr):
    xn_scr[...] = _rms(h, g_pre).astype(BF16)
    acc_scr[...] = jnp.zeros_like(acc_scr)

    def body(c, carry):
        xn = xn_scr[...]
        gate = jnp.dot(xn, wg_ref[c], preferred_element_type=F32)
        up = jnp.dot(xn, wu_ref[c], preferred_element_type=F32)
        act = (jax.nn.silu(gate) * up).astype(BF16)
        acc_scr[...] += jnp.dot(act, wd_ref[c], preferred_element_type=F32)
        return carry

    lax.fori_loop(0, N_FF_CHUNKS, body, 0)
    return acc_scr[...]


def _ffn_qkv_kernel(h_ref, g_ref, wg_ref, wu_ref, wd_ref, win_ref,
                    hout_ref, qa_ref, ka_ref, va_ref, qb_ref, kb_ref, vb_ref,
                    xn_scr, acc_scr):
    h = h_ref[...]
    f = _swiglu(h, _row(g_ref, N_FFN1_PRE), wg_ref, wu_ref, wd_ref, xn_scr, acc_scr)
    h1 = h + 0.5 * _rms(f, _row(g_ref, N_FFN1_POST))
    hout_ref[...] = h1
    u = _rms(h1, _row(g_ref, N_MIX_PRE)).astype(BF16)
    z = jnp.dot(u, win_ref[...], preferred_element_type=F32).astype(BF16)
    o = 0
    for ref, w in ((qa_ref, QA_W), (ka_ref, KVA_W), (va_ref, KVA_W),
                   (qb_ref, QB_W), (kb_ref, QB_W), (vb_ref, QB_W)):
        ref[...] = z[:, o:o + w]
        o += w


def _ffn_ple_kernel(h_ref, p_ref, g_ref, wg_ref, wu_ref, wd_ref, wpg_ref, wpp_ref,
                    hout_ref, xn_scr, acc_scr):
    h = h_ref[...]
    f = _swiglu(h, _row(g_ref, N_FFN2_PRE), wg_ref, wu_ref, wd_ref, xn_scr, acc_scr)
    h2 = h + 0.5 * _rms(f, _row(g_ref, N_FFN2_POST))
    u = _rms(h2, _row(g_ref, N_PLE_PRE)).astype(BF16)
    gate = jax.nn.sigmoid(jnp.dot(u, wpg_ref[...], preferred_element_type=F32))
    proj = jnp.dot(p_ref[...].astype(BF16), wpp_ref[...], preferred_element_type=F32)
    hout_ref[...] = h2 + _rms(proj * gate, _row(g_ref, N_PLE_POST))


def _attn_kernel(sink_ref, h_ref, qa_ref, kap_ref, kac_ref, kan_ref, vap_ref, vac_ref, van_ref,
                 qb_ref, kbp_ref, kbc_ref, kbn_ref, vbp_ref, vbc_ref, vbn_ref,
                 alibi_ref, nbias_ref, g_ref, gga_ref, ggb_ref, wout_ref,
                 hout_ref, ka_h, va_h, kb_h, vb_h, o_scr, *, seq_len):
    t = pl.program_id(1)
    n_blocks = seq_len // BLK
    n_rows = seq_len // GRID_W

    ka_h[0:WA_HALO] = kap_ref[...]
    ka_h[WA_HALO:WA_HALO + ATT_TILE] = kac_ref[...]
    ka_h[WA_HALO + ATT_TILE:] = kan_ref[...]
    va_h[0:WA_HALO] = vap_ref[...]
    va_h[WA_HALO:WA_HALO + ATT_TILE] = vac_ref[...]
    va_h[WA_HALO + ATT_TILE:] = van_ref[...]
    kb_h[0:NA_HALO] = kbp_ref[...]
    kb_h[NA_HALO:NA_HALO + ATT_TILE] = kbc_ref[...]
    kb_h[NA_HALO + ATT_TILE:] = kbn_ref[...]
    vb_h[0:NA_HALO] = vbp_ref[...]
    vb_h[NA_HALO:NA_HALO + ATT_TILE] = vbc_ref[...]
    vb_h[NA_HALO + ATT_TILE:] = vbn_ref[...]

    def lower_half(rows):
        return lax.broadcasted_iota(jnp.int32, (rows, LANES), 1) < HEAD_DIM

    def wa_block(j, carry):
        r0 = pl.multiple_of(j * BLK, BLK)
        q = qa_ref[pl.ds(r0, BLK), :]
        kband = ka_h[pl.ds(r0, BAND), :]
        vband = va_h[pl.ds(r0, BAND), :]
        blk = t * WA_BLOCKS + j
        var = jnp.where(blk == 0, 1, jnp.where(blk == n_blocks - 1, 2, 0))
        lo = lower_half(BLK)
        zero = jnp.zeros((BLK, LANES), BF16)
        parts = []
        for kv in range(A_KV_HEADS):
            for g in range(A_GROUP):
                slot = q[:, g * LANES:(g + 1) * LANES]
                parts.append(jnp.where(lo, slot, zero) if kv == 0 else jnp.where(lo, zero, slot))
        qm = jnp.concatenate(parts, axis=0)
        s = lax.dot_general(qm, kband, NT_DIMS, preferred_element_type=F32)
        es, rs = [], []
        for hq in range(A_HEADS):
            sh = s[hq * BLK:(hq + 1) * BLK] + alibi_ref[var, hq]
            sk = sink_ref[hq]
            m = jnp.maximum(jnp.max(sh, axis=-1, keepdims=True), sk)
            e = jnp.exp(sh - m)
            l = jnp.sum(e, axis=-1, keepdims=True) + jnp.exp(sk - m)
            es.append(e.astype(BF16))
            rs.append(1.0 / l)
        o2 = jnp.dot(jnp.concatenate(es, axis=0), vband, preferred_element_type=F32)
        for g in range(A_GROUP):
            o_lo = o2[g * BLK:(g + 1) * BLK] * rs[g]
            o_hi = o2[(A_GROUP + g) * BLK:(A_GROUP + g + 1) * BLK] * rs[A_GROUP + g]
            o_scr[pl.ds(r0, BLK), g * LANES:(g + 1) * LANES] = jnp.where(lo, o_lo, o_hi)
        return carry

    lax.fori_loop(0, WA_BLOCKS, wa_block, 0)

    def na_row(i, carry):
        r = t * NA_TILE_ROWS + i
        start = jnp.clip(r - NA_ROWS // 2, 0, n_rows - NA_ROWS)
        var = r - start
        q0 = pl.multiple_of(i * GRID_W, GRID_W)
        k0 = pl.multiple_of((start - t * NA_TILE_ROWS + NA_ROWS // 2) * GRID_W, GRID_W)
        lo = lower_half(GRID_W)
        zero = jnp.zeros((GRID_W, LANES), BF16)
        for p in range(N_PAIRS):
            cols = slice(p * LANES, (p + 1) * LANES)
            q2 = qb_ref[pl.ds(q0, GRID_W), cols]
            qm = jnp.concatenate([jnp.where(lo, q2, zero), jnp.where(lo, zero, q2)], axis=0)
            k2 = kb_h[pl.ds(k0, NA_KEYS), cols]
            v2 = vb_h[pl.ds(k0, NA_KEYS), cols]
            s = lax.dot_general(qm, k2, NT_DIMS, preferred_element_type=F32) + nbias_ref[var, p]
            m = jnp.max(s, axis=-1, keepdims=True)
            e = jnp.exp(s - m)
            l = jnp.sum(e, axis=-1, keepdims=True)
            o2 = jnp.dot(e.astype(BF16), v2, preferred_element_type=F32) * (1.0 / l)
            o_scr[pl.ds(q0, GRID_W), QA_W + p * LANES:QA_W + (p + 1) * LANES] = (
                jnp.where(lo, o2[:GRID_W], o2[GRID_W:]))
        return carry

    lax.fori_loop(0, NA_TILE_ROWS, na_row, 0)

    o = o_scr[...]
    na = _rms(o[:, :QA_W], gga_ref[...]).astype(BF16)
    nb = _rms(o[:, QA_W:], ggb_ref[...]).astype(BF16)
    mix = jnp.dot(jnp.concatenate([na, nb], axis=-1), wout_ref[...], preferred_element_type=F32)
    hout_ref[...] = h_ref[...] + _rms(mix, _row(g_ref, N_MIX_POST))


def _const_spec(shape):
    nd = len(shape)
    return pl.BlockSpec(shape, lambda *_: (0,) * nd, pipeline_mode=pl.Buffered(1))


def _params(n_grid_dims):
    return pltpu.CompilerParams(dimension_semantics=("parallel",) * n_grid_dims,
                                vmem_limit_bytes=VMEM_LIMIT_BYTES)


def _ffn_scratch():
    return [pltpu.VMEM((TOKEN_TILE, D_MODEL), BF16), pltpu.VMEM((TOKEN_TILE, D_MODEL), F32)]


def _ffn_weight_specs():
    return [_const_spec((N_FF_CHUNKS, D_MODEL, FF_CHUNK)),
            _const_spec((N_FF_CHUNKS, D_MODEL, FF_CHUNK)),
            _const_spec((N_FF_CHUNKS, FF_CHUNK, D_MODEL))]


def _ffn_qkv(h, g, wg, wu, wd, win):
    n_tok = h.shape[0]
    tile = lambda w: pl.BlockSpec((TOKEN_TILE, w), lambda i: (i, 0))
    widths = (QA_W, KVA_W, KVA_W, QB_W, QB_W, QB_W)
    return pl.pallas_call(
        _ffn_qkv_kernel,
        grid=(n_tok // TOKEN_TILE,),
        in_specs=[tile(D_MODEL), _const_spec((N_NORMS, D_MODEL))] + _ffn_weight_specs()
                 + [_const_spec((D_MODEL, IN_WIDTH))],
        out_specs=[tile(D_MODEL)] + [tile(w) for w in widths],
        out_shape=[jax.ShapeDtypeStruct((n_tok, D_MODEL), F32)]
                  + [jax.ShapeDtypeStruct((n_tok, w), BF16) for w in widths],
        scratch_shapes=_ffn_scratch(),
        compiler_params=_params(1),
        name="ffn1_qkv",
    )(h, g, wg, wu, wd, win)


def _ffn_ple(h, p, g, wg, wu, wd, wpg, wpp):
    n_tok = h.shape[0]
    tile = lambda w: pl.BlockSpec((TOKEN_TILE, w), lambda i: (i, 0))
    return pl.pallas_call(
        _ffn_ple_kernel,
        grid=(n_tok // TOKEN_TILE,),
        in_specs=[tile(D_MODEL), tile(PLE_DIM), _const_spec((N_NORMS, D_MODEL))]
                 + _ffn_weight_specs()
                 + [_const_spec((D_MODEL, D_MODEL)), _const_spec((PLE_DIM, D_MODEL))],
        out_specs=tile(D_MODEL),
        out_shape=jax.ShapeDtypeStruct((n_tok, D_MODEL), F32),
        scratch_shapes=_ffn_scratch(),
        compiler_params=_params(1),
        name="ffn2_ple",
    )(h, p, g, wg, wu, wd, wpg, wpp)


def _attention(h, qa, ka, va, qb, kb, vb, sink, alibi, nbias, g, gga, ggb, wout):
    n_batch, seq_len, _ = h.shape
    assert seq_len % ATT_TILE == 0 and seq_len // BLK >= 2 and seq_len // GRID_W >= NA_ROWS
    n_wa = seq_len // WA_HALO
    n_na = seq_len // NA_HALO
    wa_per = ATT_TILE // WA_HALO
    na_per = ATT_TILE // NA_HALO

    def cur(w):
        return pl.BlockSpec((None, ATT_TILE, w), lambda b, t: (b, t, 0))

    def halo(rows, w, per, n, side):
        if side < 0:
            return pl.BlockSpec((None, rows, w), lambda b, t: (b, jnp.maximum(t * per - 1, 0), 0))
        return pl.BlockSpec((None, rows, w), lambda b, t: (b, jnp.minimum((t + 1) * per, n - 1), 0))

    def triple(rows, w, per, n):
        return [halo(rows, w, per, n, -1), cur(w), halo(rows, w, per, n, 1)]

    in_specs = ([pl.BlockSpec(memory_space=pltpu.SMEM), cur(D_MODEL), cur(QA_W)]
                + triple(WA_HALO, KVA_W, wa_per, n_wa) + triple(WA_HALO, KVA_W, wa_per, n_wa)
                + [cur(QB_W)]
                + triple(NA_HALO, QB_W, na_per, n_na) + triple(NA_HALO, QB_W, na_per, n_na)
                + [_const_spec(alibi.shape), _const_spec(nbias.shape),
                   _const_spec((N_NORMS, D_MODEL)), _const_spec((1, QA_W)), _const_spec((1, QB_W)),
                   _const_spec((D_MODEL, D_MODEL))])
    return pl.pallas_call(
        functools.partial(_attn_kernel, seq_len=seq_len),
        grid=(n_batch, seq_len // ATT_TILE),
        in_specs=in_specs,
        out_specs=cur(D_MODEL),
        out_shape=jax.ShapeDtypeStruct(h.shape, F32),
        scratch_shapes=[pltpu.VMEM((ATT_TILE + 2 * WA_HALO, KVA_W), BF16),
                        pltpu.VMEM((ATT_TILE + 2 * WA_HALO, KVA_W), BF16),
                        pltpu.VMEM((ATT_TILE + 2 * NA_HALO, QB_W), BF16),
                        pltpu.VMEM((ATT_TILE + 2 * NA_HALO, QB_W), BF16),
                        pltpu.VMEM((ATT_TILE, D_MODEL), F32)],
        compiler_params=_params(2),
        name="attention",
    )(sink, h, qa, ka, ka, ka, va, va, va, qb, kb, kb, kb, vb, vb, vb,
      alibi, nbias, g, gga, ggb, wout)


def _qa_slot_perm():
    idx = []
    for g in range(A_GROUP):
        for kv in range(A_KV_HEADS):
            head = kv * A_GROUP + g
            idx.extend(range(head * HEAD_DIM, (head + 1) * HEAD_DIM))
    return jnp.asarray(idx, jnp.int32)


def _alibi_table():
    i = jnp.arange(BLK)[:, None]
    j = jnp.arange(BAND)[None, :]
    rel = j - BLK - i
    dist = jnp.abs(rel).astype(F32)
    slopes = jnp.exp2(-(8.0 / A_HEADS) * jnp.arange(1, A_HEADS + 1, dtype=F32))
    bias = -(slopes[:, None, None] * dist[None])
    in_window = jnp.abs(rel) <= WINDOW
    key_ok = jnp.stack([j >= 0, j >= BLK, j < 2 * BLK])
    ok = in_window[None] & key_ok
    return jnp.where(ok[:, None], bias[None], -jnp.inf)


def _na_bias_table(rpb):
    var = jnp.arange(NA_ROWS)[:, None]
    j = jnp.arange(NA_ROWS)[None, :]
    dr = j + (NA_ROWS - 1) - var
    c = jnp.arange(GRID_W)
    dc = jnp.clip(c[None, :] - c[:, None], -(NA_COLS - 1), NA_COLS - 1) + (NA_COLS - 1)
    cs = jnp.clip(c - NA_COLS // 2, 0, GRID_W - NA_COLS)
    col_ok = (c[None, :] >= cs[:, None]) & (c[None, :] < cs[:, None] + NA_COLS)
    b = rpb.astype(F32)[:, dr][:, :, :, dc]
    b = jnp.where(col_ok[None, None, None], b, -jnp.inf)
    b = b.transpose(1, 0, 3, 2, 4)
    return b.reshape(NA_ROWS, N_PAIRS, 2 * GRID_W, NA_KEYS)


def _chunk_cols(w):
    return w.astype(BF16).reshape(w.shape[0], N_FF_CHUNKS, FF_CHUNK).transpose(1, 0, 2)


def _chunk_rows(w):
    return w.astype(BF16).reshape(N_FF_CHUNKS, FF_CHUNK, w.shape[1])


def _layer_weights(i, w_ffn1_gate, w_ffn1_up, w_ffn1_down, w_in, w_out, sink, rpb,
                   w_ffn2_gate, w_ffn2_up, w_ffn2_down, w_ple_proj, w_ple_gate, norm_g):
    perm = _qa_slot_perm()
    win = w_in[i]
    scale = HEAD_DIM ** -0.5
    win = jnp.concatenate([win[:, :QA_W][:, perm] * scale,
                           win[:, QA_W:QA_W + 2 * KVA_W],
                           win[:, QA_W + 2 * KVA_W:QA_W + 2 * KVA_W + QB_W] * scale,
                           win[:, QA_W + 2 * KVA_W + QB_W:]], axis=1).astype(BF16)
    g = norm_g[i]
    gg = g[N_GROUP]
    wout = w_out[i]
    wout = jnp.concatenate([wout[:QA_W][perm], wout[QA_W:]], axis=0).astype(BF16)
    return dict(
        g=g,
        ffn1=(_chunk_cols(w_ffn1_gate[i]), _chunk_cols(w_ffn1_up[i]), _chunk_rows(w_ffn1_down[i])),
        ffn2=(_chunk_cols(w_ffn2_gate[i]), _chunk_cols(w_ffn2_up[i]), _chunk_rows(w_ffn2_down[i])),
        win=win, wout=wout, sink=sink[i].astype(F32),
        gga=gg[:QA_W][perm][None, :], ggb=gg[QA_W:][None, :],
        nbias=_na_bias_table(rpb[i]),
        wpg=w_ple_gate[i].astype(BF16), wpp=w_ple_proj[i].astype(BF16))


def _layer(h, p, lw, alibi):
    n_batch, seq_len, _ = h.shape
    flat = lambda a: a.reshape(n_batch * seq_len, a.shape[-1])
    seq = lambda a: a.reshape(n_batch, seq_len, a.shape[-1])
    h1, qa, ka, va, qb, kb, vb = _ffn_qkv(flat(h), lw["g"], *lw["ffn1"], lw["win"])
    h2 = _attention(seq(h1), seq(qa), seq(ka), seq(va), seq(qb), seq(kb), seq(vb),
                    lw["sink"], alibi, lw["nbias"], lw["g"], lw["gga"], lw["ggb"], lw["wout"])
    h3 = _ffn_ple(flat(h2), flat(p), lw["g"], *lw["ffn2"], lw["wpg"], lw["wpp"])
    return seq(h3)


def kernel(x_prompt, x_sample, p_prompt, p_sample, w_ffn1_gate, w_ffn1_up, w_ffn1_down,
           w_in, w_out, sink, rpb, w_ffn2_gate, w_ffn2_up, w_ffn2_down,
           w_ple_proj, w_ple_gate, norm_g):
    alibi = _alibi_table()
    hp, hs = x_prompt, x_sample
    for i in range(DEPTH):
        lw = _layer_weights(i, w_ffn1_gate, w_ffn1_up, w_ffn1_down, w_in, w_out, sink, rpb,
                            w_ffn2_gate, w_ffn2_up, w_ffn2_down, w_ple_proj, w_ple_gate, norm_g)
        hp = _layer(hp, p_prompt[i], lw, alibi)
        hs = _layer(hs, p_sample[i], lw, alibi)
    return (hp, hs)
```

```python
import functools

import jax
import jax.numpy as jnp
from jax import lax
from jax.experimental import pallas as pl
from jax.experimental.pallas import tpu as pltpu

D_MODEL = 1024
DEPTH = 4
HEAD_DIM = 64
A_HEADS = 8
A_KV_HEADS = 2
A_GROUP = A_HEADS // A_KV_HEADS
B_HEADS = 8
WINDOW = 128
BLK = 128
GRID_W = 64
NA_ROWS = 8
NA_COLS = 16
D_FF = 2816
PLE_DIM = 256
EPS = 1e-6
QA_W = A_HEADS * HEAD_DIM
KVA_W = A_KV_HEADS * HEAD_DIM
QB_W = B_HEADS * HEAD_DIM
IN_WIDTH = QA_W + 2 * KVA_W + 3 * QB_W
(N_FFN1_PRE, N_FFN1_POST, N_MIX_PRE, N_MIX_POST, N_FFN2_PRE, N_FFN2_POST,
 N_PLE_PRE, N_PLE_POST, N_GROUP) = range(9)
N_NORMS = 9

LANES = 128
MXU_COLS = 256
VMEM_LIMIT_BYTES = 56 * 1024 * 1024

FF_CHUNK = MXU_COLS
N_FF_CHUNKS = D_FF // FF_CHUNK
TOKEN_TILE = 512
ATT_TILE = 512
WA_BLOCKS = ATT_TILE // BLK
NA_TILE_ROWS = ATT_TILE // GRID_W
WA_HALO = BLK
NA_HALO = (NA_ROWS // 2) * GRID_W
BAND = 3 * BLK
NA_KEYS = NA_ROWS * GRID_W
N_PAIRS = B_HEADS // 2

assert D_FF % FF_CHUNK == 0 and ATT_TILE % BLK == 0 and ATT_TILE % GRID_W == 0
assert 2 * HEAD_DIM == LANES and NA_TILE_ROWS == NA_ROWS

F32 = jnp.float32
BF16 = jnp.bfloat16
NT_DIMS = (((1,), (1,)), ((), ()))


def _rms(x, g):
    ms = jnp.mean(x * x, axis=-1, keepdims=True)
    return x * lax.rsqrt(ms + EPS) * g


def _row(g_ref, k):
    return g_ref[k:k + 1, :]


def _swiglu(h, g_pre, wg_ref, wu_ref, wd_ref, xn_scr, acc_scr):
    xn_scr[...] = _rms(h, g_pre).astype(BF16)
    acc_scr[...] = jnp.zeros_like(acc_scr)

    def body(c, carry):
        xn = xn_scr[...]
        gate = jnp.dot(xn, wg_ref[c], preferred_element_type=F32)
        up = jnp.dot(xn, wu_ref[c], preferred_element_type=F32)
        act = (jax.nn.silu(gate) * up).astype(BF16)
        acc_scr[...] += jnp.dot(act, wd_ref[c], preferred_element_type=F32)
        return carry

    lax.fori_loop(0, N_FF_CHUNKS, body, 0)
    return acc_scr[...]


def _ffn_qkv_kernel(h_ref, g_ref, wg_ref, wu_ref, wd_ref, win_ref,
                    hout_ref, qa_ref, ka_ref, va_ref, qb_ref, kb_ref, vb_ref,
                    xn_scr, acc_scr):
    h = h_ref[...]
    f = _swiglu(h, _row(g_ref, N_FFN1_PRE), wg_ref, wu_ref, wd_ref, xn_scr, acc_scr)
    h1 = h + 0.5 * _rms(f, _row(g_ref, N_FFN1_POST))
    hout_ref[...] = h1
    u = _rms(h1, _row(g_ref, N_MIX_PRE)).astype(BF16)
    z = jnp.dot(u, win_ref[...], preferred_element_type=F32).astype(BF16)
    o = 0
    for ref, w in ((qa_ref, QA_W), (ka_ref, KVA_W), (va_ref, KVA_W),
                   (qb_ref, QB_W), (kb_ref, QB_W), (vb_ref, QB_W)):
        ref[...] = z[:, o:o + w]
        o += w


def _ffn_ple_kernel(h_ref, p_ref, g_ref, wg_ref, wu_ref, wd_ref, wpg_ref, wpp_ref,
                    hout_ref, xn_scr, acc_scr):
    h = h_ref[...]
    f = _swiglu(h, _row(g_ref, N_FFN2_PRE), wg_ref, wu_ref, wd_ref, xn_scr, acc_scr)
    h2 = h + 0.5 * _rms(f, _row(g_ref, N_FFN2_POST))
    u = _rms(h2, _row(g_ref, N_PLE_PRE)).astype(BF16)
    gate = jax.nn.sigmoid(jnp.dot(u, wpg_ref[...], preferred_element_type=F32))
    proj = jnp.dot(p_ref[...].astype(BF16), wpp_ref[...], preferred_element_type=F32)
    hout_ref[...] = h2 + _rms(proj * gate, _row(g_ref, N_PLE_POST))


def _attn_kernel(sink_ref, h_ref, qa_ref, kap_ref, kac_ref, kan_ref, vap_ref, vac_ref, van_ref,
                 qb_ref, kbp_ref, kbc_ref, kbn_ref, vbp_ref, vbc_ref, vbn_ref,
                 alibi_ref, nbias_ref, g_ref, gga_ref, ggb_ref, wout_ref,
                 hout_ref, ka_h, va_h, kb_h, vb_h, o_scr, *, seq_len):
    t = pl.program_id(1)
    n_blocks = seq_len // BLK
    n_rows = seq_len // GRID_W

    ka_h[0:WA_HALO] = kap_ref[...]
    ka_h[WA_HALO:WA_HALO + ATT_TILE] = kac_ref[...]
    ka_h[WA_HALO + ATT_TILE:] = kan_ref[...]
    va_h[0:WA_HALO] = vap_ref[...]
    va_h[WA_HALO:WA_HALO + ATT_TILE] = vac_ref[...]
    va_h[WA_HALO + ATT_TILE:] = van_ref[...]
    kb_h[0:NA_HALO] = kbp_ref[...]
    kb_h[NA_HALO:NA_HALO + ATT_TILE] = kbc_ref[...]
    kb_h[NA_HALO + ATT_TILE:] = kbn_ref[...]
    vb_h[0:NA_HALO] = vbp_ref[...]
    vb_h[NA_HALO:NA_HALO + ATT_TILE] = vbc_ref[...]
    vb_h[NA_HALO + ATT_TILE:] = vbn_ref[...]

    lo_blk = lax.broadcasted_iota(jnp.int32, (BLK, LANES), 1) < HEAD_DIM
    lo_row = lax.broadcasted_iota(jnp.int32, (GRID_W, LANES), 1) < HEAD_DIM

    def wa_scores(j):
        r0 = j * BLK
        q = qa_ref[r0:r0 + BLK, :]
        zero = jnp.zeros((BLK, LANES), BF16)
        parts = []
        for kv in range(A_KV_HEADS):
            for g in range(A_GROUP):
                slot = q[:, g * LANES:(g + 1) * LANES]
                parts.append(jnp.where(lo_blk, slot, zero) if kv == 0
                             else jnp.where(lo_blk, zero, slot))
        qm = jnp.concatenate(parts, axis=0)
        return lax.dot_general(qm, ka_h[r0:r0 + BAND, :], NT_DIMS, preferred_element_type=F32)

    def wa_finish(j, s):
        r0 = j * BLK
        blk = t * WA_BLOCKS + j
        var = jnp.where(blk == 0, 1, jnp.where(blk == n_blocks - 1, 2, 0))
        es, rs = [], []
        for hq in range(A_HEADS):
            sh = s[hq * BLK:(hq + 1) * BLK] + alibi_ref[var, hq]
            sk = sink_ref[hq]
            m = jnp.maximum(jnp.max(sh, axis=-1, keepdims=True), sk)
            e = jnp.exp(sh - m)
            l = jnp.sum(e, axis=-1, keepdims=True) + jnp.exp(sk - m)
            es.append(e.astype(BF16))
            rs.append(1.0 / l)
        o2 = jnp.dot(jnp.concatenate(es, axis=0), va_h[r0:r0 + BAND, :],
                     preferred_element_type=F32)
        for g in range(A_GROUP):
            o_lo = o2[g * BLK:(g + 1) * BLK] * rs[g]
            o_hi = o2[(A_GROUP + g) * BLK:(A_GROUP + g + 1) * BLK] * rs[A_GROUP + g]
            o_scr[r0:r0 + BLK, g * LANES:(g + 1) * LANES] = jnp.where(lo_blk, o_lo, o_hi)

    def na_window(i):
        r = t * NA_TILE_ROWS + i
        start = jnp.clip(r - NA_ROWS // 2, 0, n_rows - NA_ROWS)
        k0 = pl.multiple_of((start - t * NA_TILE_ROWS + NA_ROWS // 2) * GRID_W, GRID_W)
        return r - start, k0

    def na_scores(i):
        _, k0 = na_window(i)
        zero = jnp.zeros((GRID_W, LANES), BF16)
        out = []
        for p in range(N_PAIRS):
            cols = slice(p * LANES, (p + 1) * LANES)
            q2 = qb_ref[i * GRID_W:(i + 1) * GRID_W, cols]
            qm = jnp.concatenate([jnp.where(lo_row, q2, zero), jnp.where(lo_row, zero, q2)], axis=0)
            out.append(lax.dot_general(qm, kb_h[pl.ds(k0, NA_KEYS), cols], NT_DIMS,
                                       preferred_element_type=F32))
        return out

    def na_finish(i, scores):
        var, k0 = na_window(i)
        es, rs = [], []
        for p in range(N_PAIRS):
            s = scores[p] + nbias_ref[var, p]
            m = jnp.max(s, axis=-1, keepdims=True)
            e = jnp.exp(s - m)
            rs.append(1.0 / jnp.sum(e, axis=-1, keepdims=True))
            es.append(e.astype(BF16))
        for p in range(N_PAIRS):
            cols = slice(p * LANES, (p + 1) * LANES)
            o2 = jnp.dot(es[p], vb_h[pl.ds(k0, NA_KEYS), cols], preferred_element_type=F32) * rs[p]
            o_scr[i * GRID_W:(i + 1) * GRID_W, QA_W + p * LANES:QA_W + (p + 1) * LANES] = (
                jnp.where(lo_row, o2[:GRID_W], o2[GRID_W:]))

    units = []
    for j in range(WA_BLOCKS):
        units.append((wa_scores, wa_finish, j))
        for i in range(j * NA_TILE_ROWS // WA_BLOCKS, (j + 1) * NA_TILE_ROWS // WA_BLOCKS):
            units.append((na_scores, na_finish, i))
    pending = units[0][0](units[0][2])
    for k, (_, finish, idx) in enumerate(units):
        nxt = units[k + 1][0](units[k + 1][2]) if k + 1 < len(units) else None
        finish(idx, pending)
        pending = nxt

    o = o_scr[...]
    na = _rms(o[:, :QA_W], gga_ref[...]).astype(BF16)
    nb = _rms(o[:, QA_W:], ggb_ref[...]).astype(BF16)
    mix = jnp.dot(jnp.concatenate([na, nb], axis=-1), wout_ref[...], preferred_element_type=F32)
    hout_ref[...] = h_ref[...] + _rms(mix, _row(g_ref, N_MIX_POST))


def _const_spec(shape):
    nd = len(shape)
    return pl.BlockSpec(shape, lambda *_: (0,) * nd, pipeline_mode=pl.Buffered(1))


def _params(n_grid_dims):
    return pltpu.CompilerParams(dimension_semantics=("parallel",) * n_grid_dims,
                                vmem_limit_bytes=VMEM_LIMIT_BYTES)


def _ffn_scratch():
    return [pltpu.VMEM((TOKEN_TILE, D_MODEL), BF16), pltpu.VMEM((TOKEN_TILE, D_MODEL), F32)]


def _ffn_weight_specs():
    return [_const_spec((N_FF_CHUNKS, D_MODEL, FF_CHUNK)),
            _const_spec((N_FF_CHUNKS, D_MODEL, FF_CHUNK)),
            _const_spec((N_FF_CHUNKS, FF_CHUNK, D_MODEL))]


def _ffn_qkv(h, g, wg, wu, wd, win):
    n_tok = h.shape[0]
    tile = lambda w: pl.BlockSpec((TOKEN_TILE, w), lambda i: (i, 0))
    widths = (QA_W, KVA_W, KVA_W, QB_W, QB_W, QB_W)
    return pl.pallas_call(
        _ffn_qkv_kernel,
        grid=(n_tok // TOKEN_TILE,),
        in_specs=[tile(D_MODEL), _const_spec((N_NORMS, D_MODEL))] + _ffn_weight_specs()
                 + [_const_spec((D_MODEL, IN_WIDTH))],
        out_specs=[tile(D_MODEL)] + [tile(w) for w in widths],
        out_shape=[jax.ShapeDtypeStruct((n_tok, D_MODEL), F32)]
                  + [jax.ShapeDtypeStruct((n_tok, w), BF16) for w in widths],
        scratch_shapes=_ffn_scratch(),
        compiler_params=_params(1),
        name="ffn1_qkv",
    )(h, g, wg, wu, wd, win)


def _ffn_ple(h, p, g, wg, wu, wd, wpg, wpp):
    n_tok = h.shape[0]
    tile = lambda w: pl.BlockSpec((TOKEN_TILE, w), lambda i: (i, 0))
    return pl.pallas_call(
        _ffn_ple_kernel,
        grid=(n_tok // TOKEN_TILE,),
        in_specs=[tile(D_MODEL), tile(PLE_DIM), _const_spec((N_NORMS, D_MODEL))]
                 + _ffn_weight_specs()
                 + [_const_spec((D_MODEL, D_MODEL)), _const_spec((PLE_DIM, D_MODEL))],
        out_specs=tile(D_MODEL),
        out_shape=jax.ShapeDtypeStruct((n_tok, D_MODEL), F32),
        scratch_shapes=_ffn_scratch(),
        compiler_params=_params(1),
        name="ffn2_ple",
    )(h, p, g, wg, wu, wd, wpg, wpp)


def _attention(h, qa, ka, va, qb, kb, vb, sink, alibi, nbias, g, gga, ggb, wout):
    n_batch, seq_len, _ = h.shape
    assert seq_len % ATT_TILE == 0 and seq_len // BLK >= 2 and seq_len // GRID_W >= NA_ROWS
    n_wa = seq_len // WA_HALO
    n_na = seq_len // NA_HALO
    wa_per = ATT_TILE // WA_HALO
    na_per = ATT_TILE // NA_HALO

    def cur(w):
        return pl.BlockSpec((None, ATT_TILE, w), lambda b, t: (b, t, 0))

    def halo(rows, w, per, n, side):
        if side < 0:
            return pl.BlockSpec((None, rows, w), lambda b, t: (b, jnp.maximum(t * per - 1, 0), 0))
        return pl.BlockSpec((None, rows, w), lambda b, t: (b, jnp.minimum((t + 1) * per, n - 1), 0))

    def triple(rows, w, per, n):
        return [halo(rows, w, per, n, -1), cur(w), halo(rows, w, per, n, 1)]

    in_specs = ([pl.BlockSpec(memory_space=pltpu.SMEM), cur(D_MODEL), cur(QA_W)]
                + triple(WA_HALO, KVA_W, wa_per, n_wa) + triple(WA_HALO, KVA_W, wa_per, n_wa)
                + [cur(QB_W)]
                + triple(NA_HALO, QB_W, na_per, n_na) + triple(NA_HALO, QB_W, na_per, n_na)
                + [_const_spec(alibi.shape), _const_spec(nbias.shape),
                   _const_spec((N_NORMS, D_MODEL)), _const_spec((1, QA_W)), _const_spec((1, QB_W)),
                   _const_spec((D_MODEL, D_MODEL))])
    return pl.pallas_call(
        functools.partial(_attn_kernel, seq_len=seq_len),
        grid=(n_batch, seq_len // ATT_TILE),
        in_specs=in_specs,
        out_specs=cur(D_MODEL),
        out_shape=jax.ShapeDtypeStruct(h.shape, F32),
        scratch_shapes=[pltpu.VMEM((ATT_TILE + 2 * WA_HALO, KVA_W), BF16),
                        pltpu.VMEM((ATT_TILE + 2 * WA_HALO, KVA_W), BF16),
                        pltpu.VMEM((ATT_TILE + 2 * NA_HALO, QB_W), BF16),
                        pltpu.VMEM((ATT_TILE + 2 * NA_HALO, QB_W), BF16),
                        pltpu.VMEM((ATT_TILE, D_MODEL), F32)],
        compiler_params=_params(2),
        name="attention",
    )(sink, h, qa, ka, ka, ka, va, va, va, qb, kb, kb, kb, vb, vb, vb,
      alibi, nbias, g, gga, ggb, wout)


def _qa_slot_perm():
    idx = []
    for g in range(A_GROUP):
        for kv in range(A_KV_HEADS):
            head = kv * A_GROUP + g
            idx.extend(range(head * HEAD_DIM, (head + 1) * HEAD_DIM))
    return jnp.asarray(idx, jnp.int32)


def _alibi_table():
    i = jnp.arange(BLK)[:, None]
    j = jnp.arange(BAND)[None, :]
    rel = j - BLK - i
    dist = jnp.abs(rel).astype(F32)
    slopes = jnp.exp2(-(8.0 / A_HEADS) * jnp.arange(1, A_HEADS + 1, dtype=F32))
    bias = -(slopes[:, None, None] * dist[None])
    in_window = jnp.abs(rel) <= WINDOW
    key_ok = jnp.stack([j >= 0, j >= BLK, j < 2 * BLK])
    ok = in_window[None] & key_ok
    return jnp.where(ok[:, None], bias[None], -jnp.inf)


def _na_bias_table(rpb):
    var = jnp.arange(NA_ROWS)[:, None]
    j = jnp.arange(NA_ROWS)[None, :]
    dr = j + (NA_ROWS - 1) - var
    c = jnp.arange(GRID_W)
    dc = jnp.clip(c[None, :] - c[:, None], -(NA_COLS - 1), NA_COLS - 1) + (NA_COLS - 1)
    cs = jnp.clip(c - NA_COLS // 2, 0, GRID_W - NA_COLS)
    col_ok = (c[None, :] >= cs[:, None]) & (c[None, :] < cs[:, None] + NA_COLS)
    b = rpb.astype(F32)[:, dr][:, :, :, dc]
    b = jnp.where(col_ok[None, None, None], b, -jnp.inf)
    b = b.transpose(1, 0, 3, 2, 4)
    return b.reshape(NA_ROWS, N_PAIRS, 2 * GRID_W, NA_KEYS)


def _chunk_cols(w):
    return w.astype(BF16).reshape(w.shape[0], N_FF_CHUNKS, FF_CHUNK).transpose(1, 0, 2)


def _chunk_rows(w):
    return w.astype(BF16).reshape(N_FF_CHUNKS, FF_CHUNK, w.shape[1])


def _layer_weights(i, w_ffn1_gate, w_ffn1_up, w_ffn1_down, w_in, w_out, sink, rpb,
                   w_ffn2_gate, w_ffn2_up, w_ffn2_down, w_ple_proj, w_ple_gate, norm_g):
    perm = _qa_slot_perm()
    win = w_in[i]
    scale = HEAD_DIM ** -0.5
    win = jnp.concatenate([win[:, :QA_W][:, perm] * scale,
                           win[:, QA_W:QA_W + 2 * KVA_W],
                           win[:, QA_W + 2 * KVA_W:QA_W + 2 * KVA_W + QB_W] * scale,
                           win[:, QA_W + 2 * KVA_W + QB_W:]], axis=1).astype(BF16)
    g = norm_g[i]
    gg = g[N_GROUP]
    wout = w_out[i]
    wout = jnp.concatenate([wout[:QA_W][perm], wout[QA_W:]], axis=0).astype(BF16)
    return dict(
        g=g,
        ffn1=(_chunk_cols(w_ffn1_gate[i]), _chunk_cols(w_ffn1_up[i]), _chunk_rows(w_ffn1_down[i])),
        ffn2=(_chunk_cols(w_ffn2_gate[i]), _chunk_cols(w_ffn2_up[i]), _chunk_rows(w_ffn2_down[i])),
        win=win, wout=wout, sink=sink[i].astype(F32),
        gga=gg[:QA_W][perm][None, :], ggb=gg[QA_W:][None, :],
        nbias=_na_bias_table(rpb[i]),
        wpg=w_ple_gate[i].astype(BF16), wpp=w_ple_proj[i].astype(BF16))


def _layer(h, p, lw, alibi):
    n_batch, seq_len, _ = h.shape
    flat = lambda a: a.reshape(n_batch * seq_len, a.shape[-1])
    seq = lambda a: a.reshape(n_batch, seq_len, a.shape[-1])
    h1, qa, ka, va, qb, kb, vb = _ffn_qkv(flat(h), lw["g"], *lw["ffn1"], lw["win"])
    h2 = _attention(seq(h1), seq(qa), seq(ka), seq(va), seq(qb), seq(kb), seq(vb),
                    lw["sink"], alibi, lw["nbias"], lw["g"], lw["gga"], lw["ggb"], lw["wout"])
    h3 = _ffn_ple(flat(h2), flat(p), lw["g"], *lw["ffn2"], lw["wpg"], lw["wpp"])
    return seq(h3)


def kernel(x_prompt, x_sample, p_prompt, p_sample, w_ffn1_gate, w_ffn1_up, w_ffn1_down,
           w_in, w_out, sink, rpb, w_ffn2_gate, w_ffn2_up, w_ffn2_down,
           w_ple_proj, w_ple_gate, norm_g):
    alibi = _alibi_table()
    hp, hs = x_prompt, x_sample
    for i in range(DEPTH):
        lw = _layer_weights(i, w_ffn1_gate, w_ffn1_up, w_ffn1_down, w_in, w_out, sink, rpb,
                            w_ffn2_gate, w_ffn2_up, w_ffn2_down, w_ple_proj, w_ple_gate, norm_g)
        hp = _layer(hp, p_prompt[i], lw, alibi)
        hs = _layer(hs, p_sample[i], lw, alibi)
    return (hp, hs)
```

```python
import functools

import jax
import jax.numpy as jnp
from jax import lax
from jax.experimental import pallas as pl
from jax.experimental.pallas import tpu as pltpu

D_MODEL = 1024
DEPTH = 4
HEAD_DIM = 64
A_HEADS = 8
A_KV_HEADS = 2
A_GROUP = A_HEADS // A_KV_HEADS
B_HEADS = 8
WINDOW = 128
BLK = 128
GRID_W = 64
NA_ROWS = 8
NA_COLS = 16
D_FF = 2816
PLE_DIM = 256
EPS = 1e-6
QA_W = A_HEADS * HEAD_DIM
KVA_W = A_KV_HEADS * HEAD_DIM
QB_W = B_HEADS * HEAD_DIM
IN_WIDTH = QA_W + 2 * KVA_W + 3 * QB_W
(N_FFN1_PRE, N_FFN1_POST, N_MIX_PRE, N_MIX_POST, N_FFN2_PRE, N_FFN2_POST,
 N_PLE_PRE, N_PLE_POST, N_GROUP) = range(9)
N_NORMS = 9

LANES = 128
MXU_COLS = 256
VMEM_LIMIT_BYTES = 56 * 1024 * 1024

FF_CHUNK = MXU_COLS
N_FF_CHUNKS = D_FF // FF_CHUNK
TOKEN_TILE = 512
ATT_TILE = 512
WA_BLOCKS = ATT_TILE // BLK
NA_TILE_ROWS = ATT_TILE // GRID_W
WA_HALO = BLK
NA_HALO = (NA_ROWS // 2) * GRID_W
BAND = 3 * BLK
NA_KEYS = NA_ROWS * GRID_W
N_PAIRS = B_HEADS // 2

assert D_FF % FF_CHUNK == 0 and ATT_TILE % BLK == 0 and ATT_TILE % GRID_W == 0
assert 2 * HEAD_DIM == LANES and NA_TILE_ROWS == NA_ROWS

F32 = jnp.float32
BF16 = jnp.bfloat16
NT_DIMS = (((1,), (1,)), ((), ()))


def _rms(x, g):
    ms = jnp.mean(x * x, axis=-1, keepdims=True)
    return x * lax.rsqrt(ms + EPS) * g


def _row(g_ref, k):
    return g_ref[k:k + 1, :]


def _swiglu(h, g_pre, wg_ref, wu_ref, wd_ref, xn_scr, acc_scr):
    xn_scr[...] = _rms(h, g_pre).astype(BF16)
    acc_scr[...] = jnp.zeros_like(acc_scr)

    def body(c, carry):
        off = pl.multiple_of(c * FF_CHUNK, FF_CHUNK)
        xn = xn_scr[...]
        gate = jnp.dot(xn, wg_ref[:, pl.ds(off, FF_CHUNK)], preferred_element_type=F32)
        up = jnp.dot(xn, wu_ref[:, pl.ds(off, FF_CHUNK)], preferred_element_type=F32)
        act = (jax.nn.silu(gate) * up).astype(BF16)
        acc_scr[...] += jnp.dot(act, wd_ref[pl.ds(off, FF_CHUNK), :], preferred_element_type=F32)
        return carry

    lax.fori_loop(0, N_FF_CHUNKS, body, 0)
    return acc_scr[...]


def _ffn_qkv_kernel(h_ref, g_ref, wg_ref, wu_ref, wd_ref, win_ref,
                    hout_ref, qa_ref, ka_ref, va_ref, qb_ref, kb_ref, vb_ref,
                    xn_scr, acc_scr):
    h = h_ref[...]
    f = _swiglu(h, _row(g_ref, N_FFN1_PRE), wg_ref, wu_ref, wd_ref, xn_scr, acc_scr)
    h1 = h + 0.5 * _rms(f, _row(g_ref, N_FFN1_POST))
    hout_ref[...] = h1
    u = _rms(h1, _row(g_ref, N_MIX_PRE)).astype(BF16)
    z = jnp.dot(u, win_ref[...], preferred_element_type=F32).astype(BF16)
    o = 0
    for ref, w in ((qa_ref, QA_W), (ka_ref, KVA_W), (va_ref, KVA_W),
                   (qb_ref, QB_W), (kb_ref, QB_W), (vb_ref, QB_W)):
        ref[...] = z[:, o:o + w]
        o += w


def _ffn_ple_kernel(h_ref, p_ref, g_ref, wg_ref, wu_ref, wd_ref, wpg_ref, wpp_ref,
                    hout_ref, xn_scr, acc_scr):
    h = h_ref[...]
    f = _swiglu(h, _row(g_ref, N_FFN2_PRE), wg_ref, wu_ref, wd_ref, xn_scr, acc_scr)
    h2 = h + 0.5 * _rms(f, _row(g_ref, N_FFN2_POST))
    u = _rms(h2, _row(g_ref, N_PLE_PRE)).astype(BF16)
    gate = jax.nn.sigmoid(jnp.dot(u, wpg_ref[...], preferred_element_type=F32))
    proj = jnp.dot(p_ref[...].astype(BF16), wpp_ref[...], preferred_element_type=F32)
    hout_ref[...] = h2 + _rms(proj * gate, _row(g_ref, N_PLE_POST))


def _attn_kernel(sink_ref, h_ref, qa_ref, kap_ref, kac_ref, kan_ref, vap_ref, vac_ref, van_ref,
                 qb_ref, kbp_ref, kbc_ref, kbn_ref, vbp_ref, vbc_ref, vbn_ref,
                 alibi_ref, nbias_ref, g_ref, gga_ref, ggb_ref, wout_ref,
                 hout_ref, ka_h, va_h, kb_h, vb_h, o_scr, *, seq_len, layer):
    t = pl.program_id(1)
    n_blocks = seq_len // BLK
    n_rows = seq_len // GRID_W

    ka_h[0:WA_HALO] = kap_ref[...]
    ka_h[WA_HALO:WA_HALO + ATT_TILE] = kac_ref[...]
    ka_h[WA_HALO + ATT_TILE:] = kan_ref[...]
    va_h[0:WA_HALO] = vap_ref[...]
    va_h[WA_HALO:WA_HALO + ATT_TILE] = vac_ref[...]
    va_h[WA_HALO + ATT_TILE:] = van_ref[...]
    kb_h[0:NA_HALO] = kbp_ref[...]
    kb_h[NA_HALO:NA_HALO + ATT_TILE] = kbc_ref[...]
    kb_h[NA_HALO + ATT_TILE:] = kbn_ref[...]
    vb_h[0:NA_HALO] = vbp_ref[...]
    vb_h[NA_HALO:NA_HALO + ATT_TILE] = vbc_ref[...]
    vb_h[NA_HALO + ATT_TILE:] = vbn_ref[...]

    lo_blk = lax.broadcasted_iota(jnp.int32, (BLK, LANES), 1) < HEAD_DIM
    lo_row = lax.broadcasted_iota(jnp.int32, (GRID_W, LANES), 1) < HEAD_DIM

    def wa_scores(j):
        r0 = j * BLK
        q = qa_ref[r0:r0 + BLK, :]
        zero = jnp.zeros((BLK, LANES), BF16)
        parts = []
        for kv in range(A_KV_HEADS):
            for g in range(A_GROUP):
                slot = q[:, g * LANES:(g + 1) * LANES]
                parts.append(jnp.where(lo_blk, slot, zero) if kv == 0
                             else jnp.where(lo_blk, zero, slot))
        qm = jnp.concatenate(parts, axis=0)
        return lax.dot_general(qm, ka_h[r0:r0 + BAND, :], NT_DIMS, preferred_element_type=F32)

    def wa_finish(j, s):
        r0 = j * BLK
        blk = t * WA_BLOCKS + j
        var = jnp.where(blk == 0, 1, jnp.where(blk == n_blocks - 1, 2, 0))
        es, rs = [], []
        for hq in range(A_HEADS):
            sh = s[hq * BLK:(hq + 1) * BLK] + alibi_ref[var, hq]
            sk = sink_ref[layer, hq]
            m = jnp.maximum(jnp.max(sh, axis=-1, keepdims=True), sk)
            e = jnp.exp(sh - m)
            l = jnp.sum(e, axis=-1, keepdims=True) + jnp.exp(sk - m)
            es.append(e.astype(BF16))
            rs.append(1.0 / l)
        o2 = jnp.dot(jnp.concatenate(es, axis=0), va_h[r0:r0 + BAND, :],
                     preferred_element_type=F32)
        for g in range(A_GROUP):
            o_lo = o2[g * BLK:(g + 1) * BLK] * rs[g]
            o_hi = o2[(A_GROUP + g) * BLK:(A_GROUP + g + 1) * BLK] * rs[A_GROUP + g]
            o_scr[r0:r0 + BLK, g * LANES:(g + 1) * LANES] = jnp.where(lo_blk, o_lo, o_hi)

    def na_window(i):
        r = t * NA_TILE_ROWS + i
        start = jnp.clip(r - NA_ROWS // 2, 0, n_rows - NA_ROWS)
        k0 = pl.multiple_of((start - t * NA_TILE_ROWS + NA_ROWS // 2) * GRID_W, GRID_W)
        return r - start, k0

    def na_scores(i):
        _, k0 = na_window(i)
        zero = jnp.zeros((GRID_W, LANES), BF16)
        out = []
        for p in range(N_PAIRS):
            cols = slice(p * LANES, (p + 1) * LANES)
            q2 = qb_ref[i * GRID_W:(i + 1) * GRID_W, cols]
            qm = jnp.concatenate([jnp.where(lo_row, q2, zero), jnp.where(lo_row, zero, q2)], axis=0)
            out.append(lax.dot_general(qm, kb_h[pl.ds(k0, NA_KEYS), cols], NT_DIMS,
                                       preferred_element_type=F32))
        return out

    def na_finish(i, scores):
        var, k0 = na_window(i)
        es, rs = [], []
        for p in range(N_PAIRS):
            s = scores[p] + nbias_ref[var, p]
            m = jnp.max(s, axis=-1, keepdims=True)
            e = jnp.exp(s - m)
            rs.append(1.0 / jnp.sum(e, axis=-1, keepdims=True))
            es.append(e.astype(BF16))
        for p in range(N_PAIRS):
            cols = slice(p * LANES, (p + 1) * LANES)
            o2 = jnp.dot(es[p], vb_h[pl.ds(k0, NA_KEYS), cols], preferred_element_type=F32) * rs[p]
            o_scr[i * GRID_W:(i + 1) * GRID_W, QA_W + p * LANES:QA_W + (p + 1) * LANES] = (
                jnp.where(lo_row, o2[:GRID_W], o2[GRID_W:]))

    units = []
    for j in range(WA_BLOCKS):
        units.append((wa_scores, wa_finish, j))
        for i in range(j * NA_TILE_ROWS // WA_BLOCKS, (j + 1) * NA_TILE_ROWS // WA_BLOCKS):
            units.append((na_scores, na_finish, i))
    pending = units[0][0](units[0][2])
    for k, (_, finish, idx) in enumerate(units):
        nxt = units[k + 1][0](units[k + 1][2]) if k + 1 < len(units) else None
        finish(idx, pending)
        pending = nxt

    o = o_scr[...]
    na = _rms(o[:, :QA_W], gga_ref[...]).astype(BF16)
    nb = _rms(o[:, QA_W:], ggb_ref[...]).astype(BF16)
    mix = jnp.dot(jnp.concatenate([na, nb], axis=-1), wout_ref[...], preferred_element_type=F32)
    hout_ref[...] = h_ref[...] + _rms(mix, _row(g_ref, N_MIX_POST))


def _layer_spec(shape, layer):
    nd = len(shape)
    return pl.BlockSpec((None,) + tuple(shape), lambda *_: (layer,) + (0,) * nd,
                        pipeline_mode=pl.Buffered(1))


def _const_spec(shape):
    nd = len(shape)
    return pl.BlockSpec(shape, lambda *_: (0,) * nd, pipeline_mode=pl.Buffered(1))


def _params(n_grid_dims):
    return pltpu.CompilerParams(dimension_semantics=("parallel",) * n_grid_dims,
                                vmem_limit_bytes=VMEM_LIMIT_BYTES)


def _ffn_scratch():
    return [pltpu.VMEM((TOKEN_TILE, D_MODEL), BF16), pltpu.VMEM((TOKEN_TILE, D_MODEL), F32)]


def _ffn_weight_specs(layer):
    return [_layer_spec((D_MODEL, D_FF), layer), _layer_spec((D_MODEL, D_FF), layer),
            _layer_spec((D_FF, D_MODEL), layer)]


def _token_tile(w):
    return pl.BlockSpec((TOKEN_TILE, w), lambda i: (i, 0))


def _ffn_qkv(layer, h, g, wg, wu, wd, win):
    n_tok = h.shape[0]
    widths = (QA_W, KVA_W, KVA_W, QB_W, QB_W, QB_W)
    return pl.pallas_call(
        _ffn_qkv_kernel,
        grid=(n_tok // TOKEN_TILE,),
        in_specs=[_token_tile(D_MODEL), _layer_spec((N_NORMS, D_MODEL), layer)]
                 + _ffn_weight_specs(layer) + [_layer_spec((D_MODEL, IN_WIDTH), layer)],
        out_specs=[_token_tile(D_MODEL)] + [_token_tile(w) for w in widths],
        out_shape=[jax.ShapeDtypeStruct((n_tok, D_MODEL), F32)]
                  + [jax.ShapeDtypeStruct((n_tok, w), BF16) for w in widths],
        scratch_shapes=_ffn_scratch(),
        compiler_params=_params(1),
        name="ffn1_qkv",
    )(h, g, wg, wu, wd, win)


def _ffn_ple(layer, h, p, g, wg, wu, wd, wpg, wpp):
    n_tok = h.shape[0]
    p_spec = pl.BlockSpec((None, TOKEN_TILE, PLE_DIM), lambda i: (layer, i, 0))
    return pl.pallas_call(
        _ffn_ple_kernel,
        grid=(n_tok // TOKEN_TILE,),
        in_specs=[_token_tile(D_MODEL), p_spec, _layer_spec((N_NORMS, D_MODEL), layer)]
                 + _ffn_weight_specs(layer)
                 + [_layer_spec((D_MODEL, D_MODEL), layer), _layer_spec((PLE_DIM, D_MODEL), layer)],
        out_specs=_token_tile(D_MODEL),
        out_shape=jax.ShapeDtypeStruct((n_tok, D_MODEL), F32),
        scratch_shapes=_ffn_scratch(),
        compiler_params=_params(1),
        name="ffn2_ple",
    )(h, p, g, wg, wu, wd, wpg, wpp)


def _attention(layer, h, qa, ka, va, qb, kb, vb, sink, alibi, nbias, g, gga, ggb, wout):
    n_batch, seq_len, _ = h.shape
    assert seq_len % ATT_TILE == 0 and seq_len // BLK >= 2 and seq_len // GRID_W >= NA_ROWS
    n_wa = seq_len // WA_HALO
    n_na = seq_len // NA_HALO
    wa_per = ATT_TILE // WA_HALO
    na_per = ATT_TILE // NA_HALO

    def cur(w):
        return pl.BlockSpec((None, ATT_TILE, w), lambda b, t: (b, t, 0))

    def halo(rows, w, per, n, side):
        if side < 0:
            return pl.BlockSpec((None, rows, w), lambda b, t: (b, jnp.maximum(t * per - 1, 0), 0))
        return pl.BlockSpec((None, rows, w), lambda b, t: (b, jnp.minimum((t + 1) * per, n - 1), 0))

    def triple(rows, w, per, n):
        return [halo(rows, w, per, n, -1), cur(w), halo(rows, w, per, n, 1)]

    in_specs = ([pl.BlockSpec(memory_space=pltpu.SMEM), cur(D_MODEL), cur(QA_W)]
                + triple(WA_HALO, KVA_W, wa_per, n_wa) + triple(WA_HALO, KVA_W, wa_per, n_wa)
                + [cur(QB_W)]
                + triple(NA_HALO, QB_W, na_per, n_na) + triple(NA_HALO, QB_W, na_per, n_na)
                + [_const_spec(alibi.shape), _layer_spec(nbias.shape[1:], layer),
                   _layer_spec((N_NORMS, D_MODEL), layer),
                   _layer_spec((1, QA_W), layer), _layer_spec((1, QB_W), layer),
                   _layer_spec((D_MODEL, D_MODEL), layer)])
    return pl.pallas_call(
        functools.partial(_attn_kernel, seq_len=seq_len, layer=layer),
        grid=(n_batch, seq_len // ATT_TILE),
        in_specs=in_specs,
        out_specs=cur(D_MODEL),
        out_shape=jax.ShapeDtypeStruct(h.shape, F32),
        scratch_shapes=[pltpu.VMEM((ATT_TILE + 2 * WA_HALO, KVA_W), BF16),
                        pltpu.VMEM((ATT_TILE + 2 * WA_HALO, KVA_W), BF16),
                        pltpu.VMEM((ATT_TILE + 2 * NA_HALO, QB_W), BF16),
                        pltpu.VMEM((ATT_TILE + 2 * NA_HALO, QB_W), BF16),
                        pltpu.VMEM((ATT_TILE, D_MODEL), F32)],
        compiler_params=_params(2),
        name="attention",
    )(sink, h, qa, ka, ka, ka, va, va, va, qb, kb, kb, kb, vb, vb, vb,
      alibi, nbias, g, gga, ggb, wout)


def _alibi_table():
    i = jnp.arange(BLK)[:, None]
    j = jnp.arange(BAND)[None, :]
    rel = j - BLK - i
    dist = jnp.abs(rel).astype(F32)
    slopes = jnp.exp2(-(8.0 / A_HEADS) * jnp.arange(1, A_HEADS + 1, dtype=F32))
    bias = -(slopes[:, None, None] * dist[None])
    in_window = jnp.abs(rel) <= WINDOW
    key_ok = jnp.stack([j >= 0, j >= BLK, j < 2 * BLK])
    ok = in_window[None] & key_ok
    return jnp.where(ok[:, None], bias[None], -jnp.inf)


def _na_bias_tables(rpb):
    c = jnp.arange(GRID_W)
    dc = jnp.clip(c[None, :] - c[:, None], -(NA_COLS - 1), NA_COLS - 1) + (NA_COLS - 1)
    cs = jnp.clip(c - NA_COLS // 2, 0, GRID_W - NA_COLS)
    col_ok = (c[None, :] >= cs[:, None]) & (c[None, :] < cs[:, None] + NA_COLS)
    onehot = (dc[None] == jnp.arange(2 * NA_COLS - 1)[:, None, None]).astype(F32)
    b = jnp.einsum("lhrc,cqk->lhrqk", rpb.astype(F32), onehot, precision=lax.Precision.HIGHEST)
    b = jnp.where(col_ok, b, -jnp.inf)
    b = jnp.stack([b[:, :, NA_ROWS - 1 - v:2 * NA_ROWS - 1 - v] for v in range(NA_ROWS)], axis=1)
    b = b.transpose(0, 1, 2, 4, 3, 5)
    return b.reshape(DEPTH, NA_ROWS, N_PAIRS, 2 * GRID_W, NA_KEYS)


def _slot_order(a, axis):
    shape = a.shape
    a = a.reshape(shape[:axis] + (A_KV_HEADS, A_GROUP, HEAD_DIM) + shape[axis + 1:])
    a = jnp.swapaxes(a, axis, axis + 1)
    return a.reshape(shape)


def _prepare_weights(w_ffn1_gate, w_ffn1_up, w_ffn1_down, w_in, w_out, sink, rpb,
                     w_ffn2_gate, w_ffn2_up, w_ffn2_down, w_ple_proj, w_ple_gate, norm_g):
    scale = HEAD_DIM ** -0.5
    kv_end = QA_W + 2 * KVA_W
    win = jnp.concatenate([_slot_order(w_in[:, :, :QA_W], 2) * scale,
                           w_in[:, :, QA_W:kv_end],
                           w_in[:, :, kv_end:kv_end + QB_W] * scale,
                           w_in[:, :, kv_end + QB_W:]], axis=2).astype(BF16)
    wout = jnp.concatenate([_slot_order(w_out[:, :QA_W], 1), w_out[:, QA_W:]], axis=1).astype(BF16)
    gg = norm_g[:, N_GROUP]
    return dict(
        g=norm_g.astype(F32),
        ffn1=(w_ffn1_gate.astype(BF16), w_ffn1_up.astype(BF16), w_ffn1_down.astype(BF16)),
        ffn2=(w_ffn2_gate.astype(BF16), w_ffn2_up.astype(BF16), w_ffn2_down.astype(BF16)),
        win=win, wout=wout, sink=sink.astype(F32),
        gga=_slot_order(gg[:, :QA_W], 1)[:, None, :], ggb=gg[:, None, QA_W:],
        nbias=_na_bias_tables(rpb), alibi=_alibi_table(),
        wpg=w_ple_gate.astype(BF16), wpp=w_ple_proj.astype(BF16))


def _layer(layer, h, p, w):
    n_batch, seq_len, _ = h.shape
    flat = lambda a: a.reshape(n_batch * seq_len, a.shape[-1])
    seq = lambda a: a.reshape(n_batch, seq_len, a.shape[-1])
    h1, qa, ka, va, qb, kb, vb = _ffn_qkv(layer, flat(h), w["g"], *w["ffn1"], w["win"])
    h2 = _attention(layer, seq(h1), seq(qa), seq(ka), seq(va), seq(qb), seq(kb), seq(vb),
                    w["sink"], w["alibi"], w["nbias"], w["g"], w["gga"], w["ggb"], w["wout"])
    h3 = _ffn_ple(layer, flat(h2), p, w["g"], *w["ffn2"], w["wpg"], w["wpp"])
    return seq(h3)


def kernel(x_prompt, x_sample, p_prompt, p_sample, w_ffn1_gate, w_ffn1_up, w_ffn1_down,
           w_in, w_out, sink, rpb, w_ffn2_gate, w_ffn2_up, w_ffn2_down,
           w_ple_proj, w_ple_gate, norm_g):
    w = _prepare_weights(w_ffn1_gate, w_ffn1_up, w_ffn1_down, w_in, w_out, sink, rpb,
                         w_ffn2_gate, w_ffn2_up, w_ffn2_down, w_ple_proj, w_ple_gate, norm_g)
    pp = p_prompt.reshape(DEPTH, -1, PLE_DIM)
    ps = p_sample.reshape(DEPTH, -1, PLE_DIM)
    hp, hs = x_prompt, x_sample
    for layer in range(DEPTH):
        hp = _layer(layer, hp, pp, w)
        hs = _layer(layer, hs, ps, w)
    return (hp, hs)
```

```python
import functools

import jax
import jax.numpy as jnp
from jax import lax
from jax.experimental import pallas as pl
from jax.experimental.pallas import tpu as pltpu

D_MODEL = 1024
DEPTH = 4
HEAD_DIM = 64
A_HEADS = 8
A_KV_HEADS = 2
A_GROUP = A_HEADS // A_KV_HEADS
B_HEADS = 8
WINDOW = 128
BLK = 128
GRID_W = 64
NA_ROWS = 8
NA_COLS = 16
D_FF = 2816
PLE_DIM = 256
EPS = 1e-6
QA_W = A_HEADS * HEAD_DIM
KVA_W = A_KV_HEADS * HEAD_DIM
QB_W = B_HEADS * HEAD_DIM
IN_WIDTH = QA_W + 2 * KVA_W + 3 * QB_W
(N_FFN1_PRE, N_FFN1_POST, N_MIX_PRE, N_MIX_POST, N_FFN2_PRE, N_FFN2_POST,
 N_PLE_PRE, N_PLE_POST, N_GROUP) = range(9)
N_NORMS = 9

LANES = 128
MXU_COLS = 256
VMEM_LIMIT_BYTES = 56 * 1024 * 1024

FF_CHUNK = MXU_COLS
N_FF_CHUNKS = D_FF // FF_CHUNK
TOKEN_TILE = 512
ATT_TILE = 512
WA_BLOCKS = ATT_TILE // BLK
NA_TILE_ROWS = ATT_TILE // GRID_W
WA_HALO = BLK
NA_HALO = (NA_ROWS // 2) * GRID_W
BAND = 3 * BLK
NA_KEYS = NA_ROWS * GRID_W
N_PAIRS = B_HEADS // 2
_EPILOGUE_ROWS = tuple(slice(r, r + TOKEN_TILE // 2) for r in (0, TOKEN_TILE // 2))

assert D_FF % FF_CHUNK == 0 and ATT_TILE % BLK == 0 and ATT_TILE % GRID_W == 0
assert 2 * HEAD_DIM == LANES and NA_TILE_ROWS == NA_ROWS

F32 = jnp.float32
BF16 = jnp.bfloat16
NT_DIMS = (((1,), (1,)), ((), ()))


def _rms(x, g):
    ms = jnp.mean(x * x, axis=-1, keepdims=True)
    return x * lax.rsqrt(ms + EPS) * g


def _row(g_ref, k):
    return g_ref[k:k + 1, :]


def _swiglu(h, g_pre, wg_ref, wu_ref, wd_ref, xn_scr, acc_scr):
    xn_scr[...] = _rms(h, g_pre).astype(BF16)
    acc_scr[...] = jnp.zeros_like(acc_scr)

    def hidden(c):
        cols = slice(c * FF_CHUNK, (c + 1) * FF_CHUNK)
        xn = xn_scr[...]
        gate = jnp.dot(xn, wg_ref[:, cols], preferred_element_type=F32)
        up = jnp.dot(xn, wu_ref[:, cols], preferred_element_type=F32)
        return (jax.nn.silu(gate) * up).astype(BF16)

    act = hidden(0)
    for c in range(N_FF_CHUNKS):
        nxt = hidden(c + 1) if c + 1 < N_FF_CHUNKS else None
        acc_scr[...] += jnp.dot(act, wd_ref[c * FF_CHUNK:(c + 1) * FF_CHUNK, :],
                                preferred_element_type=F32)
        act = nxt


def _ffn_qkv_kernel(h_ref, g_ref, wg_ref, wu_ref, wd_ref, win_ref,
                    hout_ref, qa_ref, ka_ref, va_ref, qb_ref, kb_ref, vb_ref,
                    xn_scr, acc_scr):
    _swiglu(h_ref[...], _row(g_ref, N_FFN1_PRE), wg_ref, wu_ref, wd_ref, xn_scr, acc_scr)
    for rows in _EPILOGUE_ROWS:
        h1 = h_ref[rows, :] + 0.5 * _rms(acc_scr[rows, :], _row(g_ref, N_FFN1_POST))
        hout_ref[rows, :] = h1
        u = _rms(h1, _row(g_ref, N_MIX_PRE)).astype(BF16)
        z = jnp.dot(u, win_ref[...], preferred_element_type=F32).astype(BF16)
        o = 0
        for ref, w in ((qa_ref, QA_W), (ka_ref, KVA_W), (va_ref, KVA_W),
                       (qb_ref, QB_W), (kb_ref, QB_W), (vb_ref, QB_W)):
            ref[rows, :] = z[:, o:o + w]
            o += w


def _ffn_ple_kernel(h_ref, p_ref, g_ref, wg_ref, wu_ref, wd_ref, wpg_ref, wpp_ref,
                    hout_ref, xn_scr, acc_scr):
    _swiglu(h_ref[...], _row(g_ref, N_FFN2_PRE), wg_ref, wu_ref, wd_ref, xn_scr, acc_scr)
    for rows in _EPILOGUE_ROWS:
        h2 = h_ref[rows, :] + 0.5 * _rms(acc_scr[rows, :], _row(g_ref, N_FFN2_POST))
        u = _rms(h2, _row(g_ref, N_PLE_PRE)).astype(BF16)
        gate = jax.nn.sigmoid(jnp.dot(u, wpg_ref[...], preferred_element_type=F32))
        proj = jnp.dot(p_ref[rows, :].astype(BF16), wpp_ref[...], preferred_element_type=F32)
        hout_ref[rows, :] = h2 + _rms(proj * gate, _row(g_ref, N_PLE_POST))


def _attn_kernel(sink_ref, h_ref, qa_ref, kap_ref, kac_ref, kan_ref, vap_ref, vac_ref, van_ref,
                 qb_ref, kbp_ref, kbc_ref, kbn_ref, vbp_ref, vbc_ref, vbn_ref,
                 alibi_ref, nbias_ref, g_ref, gga_ref, ggb_ref, wout_ref,
                 hout_ref, ka_h, va_h, kb_h, vb_h, o_scr, *, seq_len, layer):
    t = pl.program_id(1)
    n_blocks = seq_len // BLK
    n_rows = seq_len // GRID_W

    ka_h[0:WA_HALO] = kap_ref[...]
    ka_h[WA_HALO:WA_HALO + ATT_TILE] = kac_ref[...]
    ka_h[WA_HALO + ATT_TILE:] = kan_ref[...]
    va_h[0:WA_HALO] = vap_ref[...]
    va_h[WA_HALO:WA_HALO + ATT_TILE] = vac_ref[...]
    va_h[WA_HALO + ATT_TILE:] = van_ref[...]
    kb_h[0:NA_HALO] = kbp_ref[...]
    kb_h[NA_HALO:NA_HALO + ATT_TILE] = kbc_ref[...]
    kb_h[NA_HALO + ATT_TILE:] = kbn_ref[...]
    vb_h[0:NA_HALO] = vbp_ref[...]
    vb_h[NA_HALO:NA_HALO + ATT_TILE] = vbc_ref[...]
    vb_h[NA_HALO + ATT_TILE:] = vbn_ref[...]

    lo_blk = lax.broadcasted_iota(jnp.int32, (BLK, LANES), 1) < HEAD_DIM
    lo_row = lax.broadcasted_iota(jnp.int32, (GRID_W, LANES), 1) < HEAD_DIM

    def wa_scores(j):
        r0 = j * BLK
        q = qa_ref[r0:r0 + BLK, :]
        zero = jnp.zeros((BLK, LANES), BF16)
        parts = []
        for kv in range(A_KV_HEADS):
            for g in range(A_GROUP):
                slot = q[:, g * LANES:(g + 1) * LANES]
                parts.append(jnp.where(lo_blk, slot, zero) if kv == 0
                             else jnp.where(lo_blk, zero, slot))
        qm = jnp.concatenate(parts, axis=0)
        return lax.dot_general(qm, ka_h[r0:r0 + BAND, :], NT_DIMS, preferred_element_type=F32)

    def wa_finish(j, s):
        r0 = j * BLK
        blk = t * WA_BLOCKS + j
        var = jnp.where(blk == 0, 1, jnp.where(blk == n_blocks - 1, 2, 0))
        es, rs = [], []
        for hq in range(A_HEADS):
            sh = s[hq * BLK:(hq + 1) * BLK] + alibi_ref[var, hq]
            sk = sink_ref[layer, hq]
            m = jnp.maximum(jnp.max(sh, axis=-1, keepdims=True), sk)
            e = jnp.exp(sh - m)
            l = jnp.sum(e, axis=-1, keepdims=True) + jnp.exp(sk - m)
            es.append(e.astype(BF16))
            rs.append(1.0 / l)
        o2 = jnp.dot(jnp.concatenate(es, axis=0), va_h[r0:r0 + BAND, :],
                     preferred_element_type=F32)
        for g in range(A_GROUP):
            o_lo = o2[g * BLK:(g + 1) * BLK] * rs[g]
            o_hi = o2[(A_GROUP + g) * BLK:(A_GROUP + g + 1) * BLK] * rs[A_GROUP + g]
            o_scr[r0:r0 + BLK, g * LANES:(g + 1) * LANES] = jnp.where(lo_blk, o_lo, o_hi)

    def na_window(i):
        r = t * NA_TILE_ROWS + i
        start = jnp.clip(r - NA_ROWS // 2, 0, n_rows - NA_ROWS)
        k0 = pl.multiple_of((start - t * NA_TILE_ROWS + NA_ROWS // 2) * GRID_W, GRID_W)
        return r - start, k0

    def na_scores(i):
        _, k0 = na_window(i)
        zero = jnp.zeros((GRID_W, LANES), BF16)
        out = []
        for p in range(N_PAIRS):
            cols = slice(p * LANES, (p + 1) * LANES)
            q2 = qb_ref[i * GRID_W:(i + 1) * GRID_W, cols]
            qm = jnp.concatenate([jnp.where(lo_row, q2, zero), jnp.where(lo_row, zero, q2)], axis=0)
            out.append(lax.dot_general(qm, kb_h[pl.ds(k0, NA_KEYS), cols], NT_DIMS,
                                       preferred_element_type=F32))
        return out

    def na_finish(i, scores):
        var, k0 = na_window(i)
        es, rs = [], []
        for p in range(N_PAIRS):
            s = scores[p] + nbias_ref[var, p]
            m = jnp.max(s, axis=-1, keepdims=True)
            e = jnp.exp(s - m)
            rs.append(1.0 / jnp.sum(e, axis=-1, keepdims=True))
            es.append(e.astype(BF16))
        for p in range(N_PAIRS):
            cols = slice(p * LANES, (p + 1) * LANES)
            o2 = jnp.dot(es[p], vb_h[pl.ds(k0, NA_KEYS), cols], preferred_element_type=F32) * rs[p]
            o_scr[i * GRID_W:(i + 1) * GRID_W, QA_W + p * LANES:QA_W + (p + 1) * LANES] = (
                jnp.where(lo_row, o2[:GRID_W], o2[GRID_W:]))

    units = []
    for j in range(WA_BLOCKS):
        units.append((wa_scores, wa_finish, j))
        for i in range(j * NA_TILE_ROWS // WA_BLOCKS, (j + 1) * NA_TILE_ROWS // WA_BLOCKS):
            units.append((na_scores, na_finish, i))
    pending = units[0][0](units[0][2])
    for k, (_, finish, idx) in enumerate(units):
        nxt = units[k + 1][0](units[k + 1][2]) if k + 1 < len(units) else None
        finish(idx, pending)
        pending = nxt

    o = o_scr[...]
    na = _rms(o[:, :QA_W], gga_ref[...]).astype(BF16)
    nb = _rms(o[:, QA_W:], ggb_ref[...]).astype(BF16)
    mix = jnp.dot(jnp.concatenate([na, nb], axis=-1), wout_ref[...], preferred_element_type=F32)
    hout_ref[...] = h_ref[...] + _rms(mix, _row(g_ref, N_MIX_POST))


def _layer_spec(shape, layer):
    nd = len(shape)
    return pl.BlockSpec((None,) + tuple(shape), lambda *_: (layer,) + (0,) * nd,
                        pipeline_mode=pl.Buffered(1))


def _const_spec(shape):
    nd = len(shape)
    return pl.BlockSpec(shape, lambda *_: (0,) * nd, pipeline_mode=pl.Buffered(1))


def _params(n_grid_dims):
    return pltpu.CompilerParams(dimension_semantics=("parallel",) * n_grid_dims,
                                vmem_limit_bytes=VMEM_LIMIT_BYTES)


def _ffn_scratch():
    return [pltpu.VMEM((TOKEN_TILE, D_MODEL), BF16), pltpu.VMEM((TOKEN_TILE, D_MODEL), F32)]


def _ffn_weight_specs(layer):
    return [_layer_spec((D_MODEL, D_FF), layer), _layer_spec((D_MODEL, D_FF), layer),
            _layer_spec((D_FF, D_MODEL), layer)]


def _token_tile(w):
    return pl.BlockSpec((TOKEN_TILE, w), lambda i: (i, 0))


def _ffn_qkv(layer, h, g, wg, wu, wd, win):
    n_tok = h.shape[0]
    widths = (QA_W, KVA_W, KVA_W, QB_W, QB_W, QB_W)
    return pl.pallas_call(
        _ffn_qkv_kernel,
        grid=(n_tok // TOKEN_TILE,),
        in_specs=[_token_tile(D_MODEL), _layer_spec((N_NORMS, D_MODEL), layer)]
                 + _ffn_weight_specs(layer) + [_layer_spec((D_MODEL, IN_WIDTH), layer)],
        out_specs=[_token_tile(D_MODEL)] + [_token_tile(w) for w in widths],
        out_shape=[jax.ShapeDtypeStruct((n_tok, D_MODEL), F32)]
                  + [jax.ShapeDtypeStruct((n_tok, w), BF16) for w in widths],
        scratch_shapes=_ffn_scratch(),
        compiler_params=_params(1),
        name="ffn1_qkv",
    )(h, g, wg, wu, wd, win)


def _ffn_ple(layer, h, p, g, wg, wu, wd, wpg, wpp):
    n_tok = h.shape[0]
    p_spec = pl.BlockSpec((None, TOKEN_TILE, PLE_DIM), lambda i: (layer, i, 0))
    return pl.pallas_call(
        _ffn_ple_kernel,
        grid=(n_tok // TOKEN_TILE,),
        in_specs=[_token_tile(D_MODEL), p_spec, _layer_spec((N_NORMS, D_MODEL), layer)]
                 + _ffn_weight_specs(layer)
                 + [_layer_spec((D_MODEL, D_MODEL), layer), _layer_spec((PLE_DIM, D_MODEL), layer)],
        out_specs=_token_tile(D_MODEL),
        out_shape=jax.ShapeDtypeStruct((n_tok, D_MODEL), F32),
        scratch_shapes=_ffn_scratch(),
        compiler_params=_params(1),
        name="ffn2_ple",
    )(h, p, g, wg, wu, wd, wpg, wpp)


def _attention(layer, h, qa, ka, va, qb, kb, vb, sink, alibi, nbias, g, gga, ggb, wout):
    n_batch, seq_len, _ = h.shape
    assert seq_len % ATT_TILE == 0 and seq_len // BLK >= 2 and seq_len // GRID_W >= NA_ROWS
    n_wa = seq_len // WA_HALO
    n_na = seq_len // NA_HALO
    wa_per = ATT_TILE // WA_HALO
    na_per = ATT_TILE // NA_HALO

    def cur(w):
        return pl.BlockSpec((None, ATT_TILE, w), lambda b, t: (b, t, 0))

    def halo(rows, w, per, n, side):
        if side < 0:
            return pl.BlockSpec((None, rows, w), lambda b, t: (b, jnp.maximum(t * per - 1, 0), 0))
        return pl.BlockSpec((None, rows, w), lambda b, t: (b, jnp.minimum((t + 1) * per, n - 1), 0))

    def triple(rows, w, per, n):
        return [halo(rows, w, per, n, -1), cur(w), halo(rows, w, per, n, 1)]

    in_specs = ([pl.BlockSpec(memory_space=pltpu.SMEM), cur(D_MODEL), cur(QA_W)]
                + triple(WA_HALO, KVA_W, wa_per, n_wa) + triple(WA_HALO, KVA_W, wa_per, n_wa)
                + [cur(QB_W)]
                + triple(NA_HALO, QB_W, na_per, n_na) + triple(NA_HALO, QB_W, na_per, n_na)
                + [_const_spec(alibi.shape), _layer_spec(nbias.shape[1:], layer),
                   _layer_spec((N_NORMS, D_MODEL), layer),
                   _layer_spec((1, QA_W), layer), _layer_spec((1, QB_W), layer),
                   _layer_spec((D_MODEL, D_MODEL), layer)])
    return pl.pallas_call(
        functools.partial(_attn_kernel, seq_len=seq_len, layer=layer),
        grid=(n_batch, seq_len // ATT_TILE),
        in_specs=in_specs,
        out_specs=cur(D_MODEL),
        out_shape=jax.ShapeDtypeStruct(h.shape, F32),
        scratch_shapes=[pltpu.VMEM((ATT_TILE + 2 * WA_HALO, KVA_W), BF16),
                        pltpu.VMEM((ATT_TILE + 2 * WA_HALO, KVA_W), BF16),
                        pltpu.VMEM((ATT_TILE + 2 * NA_HALO, QB_W), BF16),
                        pltpu.VMEM((ATT_TILE + 2 * NA_HALO, QB_W), BF16),
                        pltpu.VMEM((ATT_TILE, D_MODEL), F32)],
        compiler_params=_params(2),
        name="attention",
    )(sink, h, qa, ka, ka, ka, va, va, va, qb, kb, kb, kb, vb, vb, vb,
      alibi, nbias, g, gga, ggb, wout)


def _alibi_table():
    i = jnp.arange(BLK)[:, None]
    j = jnp.arange(BAND)[None, :]
    rel = j - BLK - i
    dist = jnp.abs(rel).astype(F32)
    slopes = jnp.exp2(-(8.0 / A_HEADS) * jnp.arange(1, A_HEADS + 1, dtype=F32))
    bias = -(slopes[:, None, None] * dist[None])
    in_window = jnp.abs(rel) <= WINDOW
    key_ok = jnp.stack([j >= 0, j >= BLK, j < 2 * BLK])
    ok = in_window[None] & key_ok
    return jnp.where(ok[:, None], bias[None], -jnp.inf)


def _na_bias_tables(rpb):
    c = jnp.arange(GRID_W)
    dc = jnp.clip(c[None, :] - c[:, None], -(NA_COLS - 1), NA_COLS - 1) + (NA_COLS - 1)
    cs = jnp.clip(c - NA_COLS // 2, 0, GRID_W - NA_COLS)
    col_ok = (c[None, :] >= cs[:, None]) & (c[None, :] < cs[:, None] + NA_COLS)
    onehot = (dc[None] == jnp.arange(2 * NA_COLS - 1)[:, None, None]).astype(F32)
    b = jnp.einsum("lhrc,cqk->lhrqk", rpb.astype(F32), onehot, precision=lax.Precision.HIGHEST)
    b = jnp.where(col_ok, b, -jnp.inf)
    b = jnp.stack([b[:, :, NA_ROWS - 1 - v:2 * NA_ROWS - 1 - v] for v in range(NA_ROWS)], axis=1)
    b = b.transpose(0, 1, 2, 4, 3, 5)
    return b.reshape(DEPTH, NA_ROWS, N_PAIRS, 2 * GRID_W, NA_KEYS)


def _slot_order(a, axis):
    shape = a.shape
    a = a.reshape(shape[:axis] + (A_KV_HEADS, A_GROUP, HEAD_DIM) + shape[axis + 1:])
    a = jnp.swapaxes(a, axis, axis + 1)
    return a.reshape(shape)


def _prepare_weights(w_ffn1_gate, w_ffn1_up, w_ffn1_down, w_in, w_out, sink, rpb,
                     w_ffn2_gate, w_ffn2_up, w_ffn2_down, w_ple_proj, w_ple_gate, norm_g):
    scale = HEAD_DIM ** -0.5
    kv_end = QA_W + 2 * KVA_W
    win = jnp.concatenate([_slot_order(w_in[:, :, :QA_W], 2) * scale,
                           w_in[:, :, QA_W:kv_end],
                           w_in[:, :, kv_end:kv_end + QB_W] * scale,
                           w_in[:, :, kv_end + QB_W:]], axis=2).astype(BF16)
    wout = jnp.concatenate([_slot_order(w_out[:, :QA_W], 1), w_out[:, QA_W:]], axis=1).astype(BF16)
    gg = norm_g[:, N_GROUP]
    return dict(
        g=norm_g.astype(F32),
        ffn1=(w_ffn1_gate.astype(BF16), w_ffn1_up.astype(BF16), w_ffn1_down.astype(BF16)),
        ffn2=(w_ffn2_gate.astype(BF16), w_ffn2_up.astype(BF16), w_ffn2_down.astype(BF16)),
        win=win, wout=wout, sink=sink.astype(F32),
        gga=_slot_order(gg[:, :QA_W], 1)[:, None, :], ggb=gg[:, None, QA_W:],
        nbias=_na_bias_tables(rpb), alibi=_alibi_table(),
        wpg=w_ple_gate.astype(BF16), wpp=w_ple_proj.astype(BF16))


def _layer(layer, h, p, w):
    n_batch, seq_len, _ = h.shape
    flat = lambda a: a.reshape(n_batch * seq_len, a.shape[-1])
    seq = lambda a: a.reshape(n_batch, seq_len, a.shape[-1])
    h1, qa, ka, va, qb, kb, vb = _ffn_qkv(layer, flat(h), w["g"], *w["ffn1"], w["win"])
    h2 = _attention(layer, seq(h1), seq(qa), seq(ka), seq(va), seq(qb), seq(kb), seq(vb),
                    w["sink"], w["alibi"], w["nbias"], w["g"], w["gga"], w["ggb"], w["wout"])
    h3 = _ffn_ple(layer, flat(h2), p, w["g"], *w["ffn2"], w["wpg"], w["wpp"])
    return seq(h3)


def kernel(x_prompt, x_sample, p_prompt, p_sample, w_ffn1_gate, w_ffn1_up, w_ffn1_down,
           w_in, w_out, sink, rpb, w_ffn2_gate, w_ffn2_up, w_ffn2_down,
           w_ple_proj, w_ple_gate, norm_g):
    w = _prepare_weights(w_ffn1_gate, w_ffn1_up, w_ffn1_down, w_in, w_out, sink, rpb,
                         w_ffn2_gate, w_ffn2_up, w_ffn2_down, w_ple_proj, w_ple_gate, norm_g)
    pp = p_prompt.reshape(DEPTH, -1, PLE_DIM)
    ps = p_sample.reshape(DEPTH, -1, PLE_DIM)
    hp, hs = x_prompt, x_sample
    for layer in range(DEPTH):
        hp = _layer(layer, hp, pp, w)
        hs = _layer(layer, hs, ps, w)
    return (hp, hs)
```

```python
import functools

import jax
import jax.numpy as jnp
from jax import lax
from jax.experimental import pallas as pl
from jax.experimental.pallas import tpu as pltpu

D_MODEL = 1024
DEPTH = 4
HEAD_DIM = 64
A_HEADS = 8
A_KV_HEADS = 2
A_GROUP = A_HEADS // A_KV_HEADS
B_HEADS = 8
WINDOW = 128
BLK = 128
GRID_W = 64
NA_ROWS = 8
NA_COLS = 16
D_FF = 2816
PLE_DIM = 256
EPS = 1e-6
LOG2E = 1.4426950408889634
QA_W = A_HEADS * HEAD_DIM
KVA_W = A_KV_HEADS * HEAD_DIM
QB_W = B_HEADS * HEAD_DIM
IN_WIDTH = QA_W + 2 * KVA_W + 3 * QB_W
(N_FFN1_PRE, N_FFN1_POST, N_MIX_PRE, N_MIX_POST, N_FFN2_PRE, N_FFN2_POST,
 N_PLE_PRE, N_PLE_POST, N_GROUP) = range(9)
N_NORMS = 9

LANES = 128
MXU_COLS = 256
VMEM_LIMIT_BYTES = 56 * 1024 * 1024

FF_CHUNK = MXU_COLS
N_FF_CHUNKS = D_FF // FF_CHUNK
TOKEN_TILE = 512
ATT_TILE = 512
WA_BLOCKS = ATT_TILE // BLK
NA_TILE_ROWS = ATT_TILE // GRID_W
WA_HALO = BLK
NA_HALO = (NA_ROWS // 2) * GRID_W
BAND = 3 * BLK
NA_KEYS = NA_ROWS * GRID_W
N_PAIRS = B_HEADS // 2
_EPILOGUE_ROWS = tuple(slice(r, r + TOKEN_TILE // 2) for r in (0, TOKEN_TILE // 2))
SCORE_LOOKAHEAD = 1

assert D_FF % FF_CHUNK == 0 and ATT_TILE % BLK == 0 and ATT_TILE % GRID_W == 0
assert 2 * HEAD_DIM == LANES and NA_TILE_ROWS == NA_ROWS

F32 = jnp.float32
BF16 = jnp.bfloat16
NT_DIMS = (((1,), (1,)), ((), ()))


def _rms(x, g):
    ms = jnp.mean(x * x, axis=-1, keepdims=True)
    return x * lax.rsqrt(ms + EPS) * g


def _row(g_ref, k):
    return g_ref[k:k + 1, :]


def _swiglu(h, g_pre, wg_ref, wu_ref, wd_ref, xn_scr, acc_scr):
    xn_scr[...] = _rms(h, g_pre).astype(BF16)
    acc_scr[...] = jnp.zeros_like(acc_scr)

    def hidden(c):
        cols = slice(c * FF_CHUNK, (c + 1) * FF_CHUNK)
        xn = xn_scr[...]
        gate = jnp.dot(xn, wg_ref[:, cols], preferred_element_type=F32)
        up = jnp.dot(xn, wu_ref[:, cols], preferred_element_type=F32)
        return (jax.nn.silu(gate) * up).astype(BF16)

    act = hidden(0)
    for c in range(N_FF_CHUNKS):
        nxt = hidden(c + 1) if c + 1 < N_FF_CHUNKS else None
        acc_scr[...] += jnp.dot(act, wd_ref[c * FF_CHUNK:(c + 1) * FF_CHUNK, :],
                                preferred_element_type=F32)
        act = nxt


def _ffn_qkv_kernel(h_ref, g_ref, wg_ref, wu_ref, wd_ref, win_ref,
                    hout_ref, qa_ref, ka_ref, va_ref, qb_ref, kb_ref, vb_ref,
                    xn_scr, acc_scr):
    _swiglu(h_ref[...], _row(g_ref, N_FFN1_PRE), wg_ref, wu_ref, wd_ref, xn_scr, acc_scr)
    for rows in _EPILOGUE_ROWS:
        h1 = h_ref[rows, :] + 0.5 * _rms(acc_scr[rows, :], _row(g_ref, N_FFN1_POST))
        hout_ref[rows, :] = h1
        u = _rms(h1, _row(g_ref, N_MIX_PRE)).astype(BF16)
        z = jnp.dot(u, win_ref[...], preferred_element_type=F32).astype(BF16)
        o = 0
        for ref, w in ((qa_ref, QA_W), (ka_ref, KVA_W), (va_ref, KVA_W),
                       (qb_ref, QB_W), (kb_ref, QB_W), (vb_ref, QB_W)):
            ref[rows, :] = z[:, o:o + w]
            o += w


def _ffn_ple_kernel(h_ref, p_ref, g_ref, wg_ref, wu_ref, wd_ref, wpg_ref, wpp_ref,
                    hout_ref, xn_scr, acc_scr):
    _swiglu(h_ref[...], _row(g_ref, N_FFN2_PRE), wg_ref, wu_ref, wd_ref, xn_scr, acc_scr)
    for rows in _EPILOGUE_ROWS:
        h2 = h_ref[rows, :] + 0.5 * _rms(acc_scr[rows, :], _row(g_ref, N_FFN2_POST))
        u = _rms(h2, _row(g_ref, N_PLE_PRE)).astype(BF16)
        gate = jax.nn.sigmoid(jnp.dot(u, wpg_ref[...], preferred_element_type=F32))
        proj = jnp.dot(p_ref[rows, :].astype(BF16), wpp_ref[...], preferred_element_type=F32)
        hout_ref[rows, :] = h2 + _rms(proj * gate, _row(g_ref, N_PLE_POST))


def _attn_kernel(sink_ref, h_ref, qa_ref, kap_ref, kac_ref, kan_ref, vap_ref, vac_ref, van_ref,
                 qb_ref, kbp_ref, kbc_ref, kbn_ref, vbp_ref, vbc_ref, vbn_ref,
                 alibi_ref, nbias_ref, g_ref, gga_ref, ggb_ref, wout_ref,
                 hout_ref, ka_h, va_h, kb_h, vb_h, o_scr, *, seq_len, layer):
    t = pl.program_id(1)
    n_blocks = seq_len // BLK
    n_rows = seq_len // GRID_W

    ka_h[0:WA_HALO] = kap_ref[...]
    ka_h[WA_HALO:WA_HALO + ATT_TILE] = kac_ref[...]
    ka_h[WA_HALO + ATT_TILE:] = kan_ref[...]
    va_h[0:WA_HALO] = vap_ref[...]
    va_h[WA_HALO:WA_HALO + ATT_TILE] = vac_ref[...]
    va_h[WA_HALO + ATT_TILE:] = van_ref[...]
    kb_h[0:NA_HALO] = kbp_ref[...]
    kb_h[NA_HALO:NA_HALO + ATT_TILE] = kbc_ref[...]
    kb_h[NA_HALO + ATT_TILE:] = kbn_ref[...]
    vb_h[0:NA_HALO] = vbp_ref[...]
    vb_h[NA_HALO:NA_HALO + ATT_TILE] = vbc_ref[...]
    vb_h[NA_HALO + ATT_TILE:] = vbn_ref[...]

    lo_blk = lax.broadcasted_iota(jnp.int32, (BLK, LANES), 1) < HEAD_DIM
    lo_row = lax.broadcasted_iota(jnp.int32, (GRID_W, LANES), 1) < HEAD_DIM

    def wa_scores(j):
        r0 = j * BLK
        q = qa_ref[r0:r0 + BLK, :]
        zero = jnp.zeros((BLK, LANES), BF16)
        parts = []
        for kv in range(A_KV_HEADS):
            for g in range(A_GROUP):
                slot = q[:, g * LANES:(g + 1) * LANES]
                parts.append(jnp.where(lo_blk, slot, zero) if kv == 0
                             else jnp.where(lo_blk, zero, slot))
        qm = jnp.concatenate(parts, axis=0)
        return lax.dot_general(qm, ka_h[r0:r0 + BAND, :], NT_DIMS, preferred_element_type=F32)

    def wa_finish(j, s):
        r0 = j * BLK
        blk = t * WA_BLOCKS + j
        var = jnp.where(blk == 0, 1, jnp.where(blk == n_blocks - 1, 2, 0))
        es, rs = [], []
        for hq in range(A_HEADS):
            sh = s[hq * BLK:(hq + 1) * BLK] + alibi_ref[var, hq]
            sk = sink_ref[layer, hq]
            m = jnp.maximum(jnp.max(sh, axis=-1, keepdims=True), sk)
            e = jnp.exp2(sh - m)
            l = jnp.sum(e, axis=-1, keepdims=True) + jnp.exp2(sk - m)
            es.append(e.astype(BF16))
            rs.append(1.0 / l)
        o2 = jnp.dot(jnp.concatenate(es, axis=0), va_h[r0:r0 + BAND, :],
                     preferred_element_type=F32)
        for g in range(A_GROUP):
            o_lo = o2[g * BLK:(g + 1) * BLK] * rs[g]
            o_hi = o2[(A_GROUP + g) * BLK:(A_GROUP + g + 1) * BLK] * rs[A_GROUP + g]
            o_scr[r0:r0 + BLK, g * LANES:(g + 1) * LANES] = jnp.where(lo_blk, o_lo, o_hi)

    def na_window(i):
        r = t * NA_TILE_ROWS + i
        start = jnp.clip(r - NA_ROWS // 2, 0, n_rows - NA_ROWS)
        k0 = pl.multiple_of((start - t * NA_TILE_ROWS + NA_ROWS // 2) * GRID_W, GRID_W)
        return r - start, k0

    def na_scores(i):
        _, k0 = na_window(i)
        zero = jnp.zeros((GRID_W, LANES), BF16)
        out = []
        for p in range(N_PAIRS):
            cols = slice(p * LANES, (p + 1) * LANES)
            q2 = qb_ref[i * GRID_W:(i + 1) * GRID_W, cols]
            qm = jnp.concatenate([jnp.where(lo_row, q2, zero), jnp.where(lo_row, zero, q2)], axis=0)
            out.append(lax.dot_general(qm, kb_h[pl.ds(k0, NA_KEYS), cols], NT_DIMS,
                                       preferred_element_type=F32))
        return out

    def na_finish(i, scores):
        var, k0 = na_window(i)
        es, rs = [], []
        for p in range(N_PAIRS):
            s = scores[p] + nbias_ref[var, p]
            m = jnp.max(s, axis=-1, keepdims=True)
            e = jnp.exp2(s - m)
            rs.append(1.0 / jnp.sum(e, axis=-1, keepdims=True))
            es.append(e.astype(BF16))
        for p in range(N_PAIRS):
            cols = slice(p * LANES, (p + 1) * LANES)
            o2 = jnp.dot(es[p], vb_h[pl.ds(k0, NA_KEYS), cols], preferred_element_type=F32) * rs[p]
            o_scr[i * GRID_W:(i + 1) * GRID_W, QA_W + p * LANES:QA_W + (p + 1) * LANES] = (
                jnp.where(lo_row, o2[:GRID_W], o2[GRID_W:]))

    def project(j, _):
        rows = slice(j * BLK, (j + 1) * BLK)
        o = o_scr[rows, :]
        na = _rms(o[:, :QA_W], gga_ref[...]).astype(BF16)
        nb = _rms(o[:, QA_W:], ggb_ref[...]).astype(BF16)
        mix = jnp.dot(jnp.concatenate([na, nb], axis=-1), wout_ref[...],
                      preferred_element_type=F32)
        hout_ref[rows, :] = h_ref[rows, :] + _rms(mix, _row(g_ref, N_MIX_POST))

    no_scores = lambda j: None
    units = []
    for j in range(WA_BLOCKS):
        units.append((wa_scores, wa_finish, j))
        for i in range(j * NA_TILE_ROWS // WA_BLOCKS, (j + 1) * NA_TILE_ROWS // WA_BLOCKS):
            units.append((na_scores, na_finish, i))
        units.append((no_scores, project, j))
    pending = [scores(idx) for scores, _, idx in units[:SCORE_LOOKAHEAD]]
    for k, (_, finish, idx) in enumerate(units):
        ahead = k + SCORE_LOOKAHEAD
        if ahead < len(units):
            pending.append(units[ahead][0](units[ahead][2]))
        finish(idx, pending.pop(0))


def _layer_spec(shape, layer):
    nd = len(shape)
    return pl.BlockSpec((None,) + tuple(shape), lambda *_: (layer,) + (0,) * nd,
                        pipeline_mode=pl.Buffered(1))


def _const_spec(shape):
    nd = len(shape)
    return pl.BlockSpec(shape, lambda *_: (0,) * nd, pipeline_mode=pl.Buffered(1))


def _params(n_grid_dims):
    return pltpu.CompilerParams(dimension_semantics=("parallel",) * n_grid_dims,
                                vmem_limit_bytes=VMEM_LIMIT_BYTES)


def _ffn_scratch():
    return [pltpu.VMEM((TOKEN_TILE, D_MODEL), BF16), pltpu.VMEM((TOKEN_TILE, D_MODEL), F32)]


def _ffn_weight_specs(layer):
    return [_layer_spec((D_MODEL, D_FF), layer), _layer_spec((D_MODEL, D_FF), layer),
            _layer_spec((D_FF, D_MODEL), layer)]


def _token_tile(w):
    return pl.BlockSpec((TOKEN_TILE, w), lambda i: (i, 0))


def _ffn_qkv(layer, h, g, wg, wu, wd, win):
    n_tok = h.shape[0]
    widths = (QA_W, KVA_W, KVA_W, QB_W, QB_W, QB_W)
    return pl.pallas_call(
        _ffn_qkv_kernel,
        grid=(n_tok // TOKEN_TILE,),
        in_specs=[_token_tile(D_MODEL), _layer_spec((N_NORMS, D_MODEL), layer)]
                 + _ffn_weight_specs(layer) + [_layer_spec((D_MODEL, IN_WIDTH), layer)],
        out_specs=[_token_tile(D_MODEL)] + [_token_tile(w) for w in widths],
        out_shape=[jax.ShapeDtypeStruct((n_tok, D_MODEL), F32)]
                  + [jax.ShapeDtypeStruct((n_tok, w), BF16) for w in widths],
        scratch_shapes=_ffn_scratch(),
        compiler_params=_params(1),
        name="ffn1_qkv",
    )(h, g, wg, wu, wd, win)


def _ffn_ple(layer, h, p, g, wg, wu, wd, wpg, wpp):
    n_tok = h.shape[0]
    p_spec = pl.BlockSpec((None, TOKEN_TILE, PLE_DIM), lambda i: (layer, i, 0))
    return pl.pallas_call(
        _ffn_ple_kernel,
        grid=(n_tok // TOKEN_TILE,),
        in_specs=[_token_tile(D_MODEL), p_spec, _layer_spec((N_NORMS, D_MODEL), layer)]
                 + _ffn_weight_specs(layer)
                 + [_layer_spec((D_MODEL, D_MODEL), layer), _layer_spec((PLE_DIM, D_MODEL), layer)],
        out_specs=_token_tile(D_MODEL),
        out_shape=jax.ShapeDtypeStruct((n_tok, D_MODEL), F32),
        scratch_shapes=_ffn_scratch(),
        compiler_params=_params(1),
        name="ffn2_ple",
    )(h, p, g, wg, wu, wd, wpg, wpp)


def _attention(layer, h, qa, ka, va, qb, kb, vb, sink, alibi, nbias, g, gga, ggb, wout):
    n_batch, seq_len, _ = h.shape
    assert seq_len % ATT_TILE == 0 and seq_len // BLK >= 2 and seq_len // GRID_W >= NA_ROWS
    n_wa = seq_len // WA_HALO
    n_na = seq_len // NA_HALO
    wa_per = ATT_TILE // WA_HALO
    na_per = ATT_TILE // NA_HALO

    def cur(w):
        return pl.BlockSpec((None, ATT_TILE, w), lambda b, t: (b, t, 0))

    def halo(rows, w, per, n, side):
        if side < 0:
            return pl.BlockSpec((None, rows, w), lambda b, t: (b, jnp.maximum(t * per - 1, 0), 0))
        return pl.BlockSpec((None, rows, w), lambda b, t: (b, jnp.minimum((t + 1) * per, n - 1), 0))

    def triple(rows, w, per, n):
        return [halo(rows, w, per, n, -1), cur(w), halo(rows, w, per, n, 1)]

    in_specs = ([pl.BlockSpec(memory_space=pltpu.SMEM), cur(D_MODEL), cur(QA_W)]
                + triple(WA_HALO, KVA_W, wa_per, n_wa) + triple(WA_HALO, KVA_W, wa_per, n_wa)
                + [cur(QB_W)]
                + triple(NA_HALO, QB_W, na_per, n_na) + triple(NA_HALO, QB_W, na_per, n_na)
                + [_const_spec(alibi.shape), _layer_spec(nbias.shape[1:], layer),
                   _layer_spec((N_NORMS, D_MODEL), layer),
                   _layer_spec((1, QA_W), layer), _layer_spec((1, QB_W), layer),
                   _layer_spec((D_MODEL, D_MODEL), layer)])
    return pl.pallas_call(
        functools.partial(_attn_kernel, seq_len=seq_len, layer=layer),
        grid=(n_batch, seq_len // ATT_TILE),
        in_specs=in_specs,
        out_specs=cur(D_MODEL),
        out_shape=jax.ShapeDtypeStruct(h.shape, F32),
        scratch_shapes=[pltpu.VMEM((ATT_TILE + 2 * WA_HALO, KVA_W), BF16),
                        pltpu.VMEM((ATT_TILE + 2 * WA_HALO, KVA_W), BF16),
                        pltpu.VMEM((ATT_TILE + 2 * NA_HALO, QB_W), BF16),
                        pltpu.VMEM((ATT_TILE + 2 * NA_HALO, QB_W), BF16),
                        pltpu.VMEM((ATT_TILE, D_MODEL), F32)],
        compiler_params=_params(2),
        name="attention",
    )(sink, h, qa, ka, ka, ka, va, va, va, qb, kb, kb, kb, vb, vb, vb,
      alibi, nbias, g, gga, ggb, wout)


def _alibi_table():
    i = jnp.arange(BLK)[:, None]
    j = jnp.arange(BAND)[None, :]
    rel = j - BLK - i
    dist = jnp.abs(rel).astype(F32)
    slopes = jnp.exp2(-(8.0 / A_HEADS) * jnp.arange(1, A_HEADS + 1, dtype=F32))
    bias = -(slopes[:, None, None] * dist[None])
    in_window = jnp.abs(rel) <= WINDOW
    key_ok = jnp.stack([j >= 0, j >= BLK, j < 2 * BLK])
    ok = in_window[None] & key_ok
    return jnp.where(ok[:, None], bias[None], -jnp.inf)


def _na_bias_tables(rpb):
    c = jnp.arange(GRID_W)
    dc = jnp.clip(c[None, :] - c[:, None], -(NA_COLS - 1), NA_COLS - 1) + (NA_COLS - 1)
    cs = jnp.clip(c - NA_COLS // 2, 0, GRID_W - NA_COLS)
    col_ok = (c[None, :] >= cs[:, None]) & (c[None, :] < cs[:, None] + NA_COLS)
    onehot = (dc[None] == jnp.arange(2 * NA_COLS - 1)[:, None, None]).astype(F32)
    b = jnp.einsum("lhrc,cqk->lhqrk", rpb.astype(F32), onehot, precision=lax.Precision.HIGHEST)
    b = jnp.where(col_ok[:, None, :], b, -jnp.inf)
    b = jnp.stack([b[:, :, :, NA_ROWS - 1 - v:2 * NA_ROWS - 1 - v] for v in range(NA_ROWS)],
                  axis=1)
    return b.reshape(DEPTH, NA_ROWS, N_PAIRS, 2 * GRID_W, NA_KEYS)


def _slot_order(a, axis):
    shape = a.shape
    a = a.reshape(shape[:axis] + (A_KV_HEADS, A_GROUP, HEAD_DIM) + shape[axis + 1:])
    a = jnp.swapaxes(a, axis, axis + 1)
    return a.reshape(shape)


def _prepare_weights(w_ffn1_gate, w_ffn1_up, w_ffn1_down, w_in, w_out, sink, rpb,
                     w_ffn2_gate, w_ffn2_up, w_ffn2_down, w_ple_proj, w_ple_gate, norm_g):
    scale = HEAD_DIM ** -0.5 * LOG2E
    kv_end = QA_W + 2 * KVA_W
    win = jnp.concatenate([_slot_order(w_in[:, :, :QA_W], 2) * scale,
                           w_in[:, :, QA_W:kv_end],
                           w_in[:, :, kv_end:kv_end + QB_W] * scale,
                           w_in[:, :, kv_end + QB_W:]], axis=2).astype(BF16)
    wout = jnp.concatenate([_slot_order(w_out[:, :QA_W], 1), w_out[:, QA_W:]], axis=1).astype(BF16)
    gg = norm_g[:, N_GROUP]
    return dict(
        g=norm_g.astype(F32),
        ffn1=(w_ffn1_gate.astype(BF16), w_ffn1_up.astype(BF16), w_ffn1_down.astype(BF16)),
        ffn2=(w_ffn2_gate.astype(BF16), w_ffn2_up.astype(BF16), w_ffn2_down.astype(BF16)),
        win=win, wout=wout, sink=sink.astype(F32) * LOG2E,
        gga=_slot_order(gg[:, :QA_W], 1)[:, None, :], ggb=gg[:, None, QA_W:],
        nbias=_na_bias_tables(rpb) * LOG2E, alibi=_alibi_table() * LOG2E,
        wpg=w_ple_gate.astype(BF16), wpp=w_ple_proj.astype(BF16))


def _layer(layer, h, p, w):
    n_batch, seq_len, _ = h.shape
    flat = lambda a: a.reshape(n_batch * seq_len, a.shape[-1])
    seq = lambda a: a.reshape(n_batch, seq_len, a.shape[-1])
    h1, qa, ka, va, qb, kb, vb = _ffn_qkv(layer, flat(h), w["g"], *w["ffn1"], w["win"])
    h2 = _attention(layer, seq(h1), seq(qa), seq(ka), seq(va), seq(qb), seq(kb), seq(vb),
                    w["sink"], w["alibi"], w["nbias"], w["g"], w["gga"], w["ggb"], w["wout"])
    h3 = _ffn_ple(layer, flat(h2), p, w["g"], *w["ffn2"], w["wpg"], w["wpp"])
    return seq(h3)


def kernel(x_prompt, x_sample, p_prompt, p_sample, w_ffn1_gate, w_ffn1_up, w_ffn1_down,
           w_in, w_out, sink, rpb, w_ffn2_gate, w_ffn2_up, w_ffn2_down,
           w_ple_proj, w_ple_gate, norm_g):
    w = _prepare_weights(w_ffn1_gate, w_ffn1_up, w_ffn1_down, w_in, w_out, sink, rpb,
                         w_ffn2_gate, w_ffn2_up, w_ffn2_down, w_ple_proj, w_ple_gate, norm_g)
    pp = p_prompt.reshape(DEPTH, -1, PLE_DIM)
    ps = p_sample.reshape(DEPTH, -1, PLE_DIM)
    hp, hs = x_prompt, x_sample
    for layer in range(DEPTH):
        hp = _layer(layer, hp, pp, w)
        hs = _layer(layer, hs, ps, w)
    return (hp, hs)
```

```python
import functools

import jax
import jax.numpy as jnp
from jax import lax
from jax.experimental import pallas as pl
from jax.experimental.pallas import tpu as pltpu

D_MODEL = 1024
DEPTH = 4
HEAD_DIM = 64
A_HEADS = 8
A_KV_HEADS = 2
A_GROUP = A_HEADS // A_KV_HEADS
B_HEADS = 8
WINDOW = 128
BLK = 128
GRID_W = 64
NA_ROWS = 8
NA_COLS = 16
D_FF = 2816
PLE_DIM = 256
EPS = 1e-6
LOG2E = 1.4426950408889634
QA_W = A_HEADS * HEAD_DIM
KVA_W = A_KV_HEADS * HEAD_DIM
QB_W = B_HEADS * HEAD_DIM
IN_WIDTH = QA_W + 2 * KVA_W + 3 * QB_W
(N_FFN1_PRE, N_FFN1_POST, N_MIX_PRE, N_MIX_POST, N_FFN2_PRE, N_FFN2_POST,
 N_PLE_PRE, N_PLE_POST, N_GROUP) = range(9)
N_NORMS = 9

LANES = 128
MXU_COLS = 256
VMEM_LIMIT_BYTES = 56 * 1024 * 1024

FF_CHUNK = MXU_COLS
N_FF_CHUNKS = D_FF // FF_CHUNK
TOKEN_TILE = 512
ATT_TILE = 512
WA_BLOCKS = ATT_TILE // BLK
NA_TILE_ROWS = ATT_TILE // GRID_W
WA_HALO = BLK
NA_HALO = (NA_ROWS // 2) * GRID_W
BAND = 3 * BLK
NA_KEYS = NA_ROWS * GRID_W
N_PAIRS = B_HEADS // 2
_EPILOGUE_ROWS = tuple(slice(r, r + TOKEN_TILE // 2) for r in (0, TOKEN_TILE // 2))
SCORE_LOOKAHEAD = 1

assert D_FF % FF_CHUNK == 0 and ATT_TILE % BLK == 0 and ATT_TILE % GRID_W == 0
assert 2 * HEAD_DIM == LANES and NA_TILE_ROWS == NA_ROWS

F32 = jnp.float32
BF16 = jnp.bfloat16
NT_DIMS = (((1,), (1,)), ((), ()))


def _rms(x, g):
    ms = jnp.mean(x * x, axis=-1, keepdims=True)
    return x * lax.rsqrt(ms + EPS) * g


def _row(g_ref, k):
    return g_ref[k:k + 1, :]


def _normed_input(h_ref, hnext_ref, g_pre, xn_scr):
    @pl.when(pl.program_id(0) == 0)
    def _():
        xn_scr[...] = _rms(h_ref[...], g_pre).astype(BF16)

    def prepare_next():
        xn_scr[...] = _rms(hnext_ref[...], g_pre).astype(BF16)

    return prepare_next


def _swiglu(wg_ref, wu_ref, wd_ref, xn_scr, acc_scr):
    def hidden(c, rows=slice(None)):
        cols = slice(c * FF_CHUNK, (c + 1) * FF_CHUNK)
        xn = xn_scr[rows, :]
        gate = jnp.dot(xn, wg_ref[:, cols], preferred_element_type=F32)
        up = jnp.dot(xn, wu_ref[:, cols], preferred_element_type=F32)
        return (jax.nn.silu(gate) * up).astype(BF16)

    def down(c, act, rows=slice(None)):
        part = jnp.dot(act, wd_ref[c * FF_CHUNK:(c + 1) * FF_CHUNK, :],
                       preferred_element_type=F32)
        if c == 0:
            acc_scr[rows, :] = part
        else:
            acc_scr[rows, :] += part

    last = N_FF_CHUNKS - 1
    first_rows, second_rows = _EPILOGUE_ROWS
    act = hidden(0)
    for c in range(last):
        nxt = hidden(c + 1) if c + 1 < last else hidden(last, first_rows)
        down(c, act)
        act = nxt
    down(last, act, first_rows)
    down(last, hidden(last, second_rows), second_rows)


def _ffn_qkv_kernel(h_ref, hnext_ref, g_ref, wg_ref, wu_ref, wd_ref, win_ref,
                    hout_ref, qa_ref, ka_ref, va_ref, qb_ref, kb_ref, vb_ref,
                    xn_scr, acc_scr):
    prepare_next = _normed_input(h_ref, hnext_ref, _row(g_ref, N_FFN1_PRE), xn_scr)
    _swiglu(wg_ref, wu_ref, wd_ref, xn_scr, acc_scr)
    for rows in _EPILOGUE_ROWS:
        h1 = h_ref[rows, :] + 0.5 * _rms(acc_scr[rows, :], _row(g_ref, N_FFN1_POST))
        hout_ref[rows, :] = h1
        u = _rms(h1, _row(g_ref, N_MIX_PRE)).astype(BF16)
        z = jnp.dot(u, win_ref[...], preferred_element_type=F32).astype(BF16)
        o = 0
        for ref, w in ((qa_ref, QA_W), (ka_ref, KVA_W), (va_ref, KVA_W),
                       (qb_ref, QB_W), (kb_ref, QB_W), (vb_ref, QB_W)):
            ref[rows, :] = z[:, o:o + w]
            o += w
    prepare_next()


def _ffn_ple_kernel(h_ref, hnext_ref, p_ref, g_ref, wg_ref, wu_ref, wd_ref, wpg_ref, wpp_ref,
                    hout_ref, xn_scr, acc_scr):
    prepare_next = _normed_input(h_ref, hnext_ref, _row(g_ref, N_FFN2_PRE), xn_scr)
    _swiglu(wg_ref, wu_ref, wd_ref, xn_scr, acc_scr)
    prepare_next()
    for rows in _EPILOGUE_ROWS:
        h2 = h_ref[rows, :] + 0.5 * _rms(acc_scr[rows, :], _row(g_ref, N_FFN2_POST))
        u = _rms(h2, _row(g_ref, N_PLE_PRE)).astype(BF16)
        gate = jax.nn.sigmoid(jnp.dot(u, wpg_ref[...], preferred_element_type=F32))
        proj = jnp.dot(p_ref[rows, :].astype(BF16), wpp_ref[...], preferred_element_type=F32)
        hout_ref[rows, :] = h2 + _rms(proj * gate, _row(g_ref, N_PLE_POST))


def _attn_kernel(sink_ref, h_ref, qa_ref, kap_ref, kac_ref, kan_ref, vap_ref, vac_ref, van_ref,
                 qb_ref, kbp_ref, kbc_ref, kbn_ref, vbp_ref, vbc_ref, vbn_ref,
                 alibi_ref, nbias_ref, g_ref, gga_ref, ggb_ref, wout_ref,
                 hout_ref, ka_h, va_h, kb_h, vb_h, o_scr, *, seq_len, layer):
    t = pl.program_id(1)
    n_blocks = seq_len // BLK
    n_rows = seq_len // GRID_W

    ka_h[0:WA_HALO] = kap_ref[...]
    ka_h[WA_HALO:WA_HALO + ATT_TILE] = kac_ref[...]
    ka_h[WA_HALO + ATT_TILE:] = kan_ref[...]
    va_h[0:WA_HALO] = vap_ref[...]
    va_h[WA_HALO:WA_HALO + ATT_TILE] = vac_ref[...]
    va_h[WA_HALO + ATT_TILE:] = van_ref[...]
    kb_h[0:NA_HALO] = kbp_ref[...]
    kb_h[NA_HALO:NA_HALO + ATT_TILE] = kbc_ref[...]
    kb_h[NA_HALO + ATT_TILE:] = kbn_ref[...]
    vb_h[0:NA_HALO] = vbp_ref[...]
    vb_h[NA_HALO:NA_HALO + ATT_TILE] = vbc_ref[...]
    vb_h[NA_HALO + ATT_TILE:] = vbn_ref[...]

    lo_blk = lax.broadcasted_iota(jnp.int32, (BLK, LANES), 1) < HEAD_DIM
    lo_row = lax.broadcasted_iota(jnp.int32, (GRID_W, LANES), 1) < HEAD_DIM

    def wa_scores(j):
        r0 = j * BLK
        q = qa_ref[r0:r0 + BLK, :]
        zero = jnp.zeros((BLK, LANES), BF16)
        parts = []
        for kv in range(A_KV_HEADS):
            for g in range(A_GROUP):
                slot = q[:, g * LANES:(g + 1) * LANES]
                parts.append(jnp.where(lo_blk, slot, zero) if kv == 0
                             else jnp.where(lo_blk, zero, slot))
        qm = jnp.concatenate(parts, axis=0)
        return lax.dot_general(qm, ka_h[r0:r0 + BAND, :], NT_DIMS, preferred_element_type=F32)

    def wa_finish(j, s):
        r0 = j * BLK
        blk = t * WA_BLOCKS + j
        var = jnp.where(blk == 0, 1, jnp.where(blk == n_blocks - 1, 2, 0))
        es, rs = [], []
        for hq in range(A_HEADS):
            sh = s[hq * BLK:(hq + 1) * BLK] + alibi_ref[var, hq]
            sk = sink_ref[layer, hq]
            m = jnp.maximum(jnp.max(sh, axis=-1, keepdims=True), sk)
            e = jnp.exp2(sh - m)
            l = jnp.sum(e, axis=-1, keepdims=True) + jnp.exp2(sk - m)
            es.append(e.astype(BF16))
            rs.append(1.0 / l)
        o2 = jnp.dot(jnp.concatenate(es, axis=0), va_h[r0:r0 + BAND, :],
                     preferred_element_type=F32)
        for g in range(A_GROUP):
            o_lo = o2[g * BLK:(g + 1) * BLK] * rs[g]
            o_hi = o2[(A_GROUP + g) * BLK:(A_GROUP + g + 1) * BLK] * rs[A_GROUP + g]
            o_scr[r0:r0 + BLK, g * LANES:(g + 1) * LANES] = jnp.where(lo_blk, o_lo, o_hi)

    def na_window(i):
        r = t * NA_TILE_ROWS + i
        start = jnp.clip(r - NA_ROWS // 2, 0, n_rows - NA_ROWS)
        k0 = pl.multiple_of((start - t * NA_TILE_ROWS + NA_ROWS // 2) * GRID_W, GRID_W)
        return r - start, k0

    def na_scores(i):
        _, k0 = na_window(i)
        zero = jnp.zeros((GRID_W, LANES), BF16)
        out = []
        for p in range(N_PAIRS):
            cols = slice(p * LANES, (p + 1) * LANES)
            q2 = qb_ref[i * GRID_W:(i + 1) * GRID_W, cols]
            qm = jnp.concatenate([jnp.where(lo_row, q2, zero), jnp.where(lo_row, zero, q2)], axis=0)
            out.append(lax.dot_general(qm, kb_h[pl.ds(k0, NA_KEYS), cols], NT_DIMS,
                                       preferred_element_type=F32))
        return out

    def na_finish(i, scores):
        var, k0 = na_window(i)
        es, rs = [], []
        for p in range(N_PAIRS):
            s = scores[p] + nbias_ref[var, p]
            m = jnp.max(s, axis=-1, keepdims=True)
            e = jnp.exp2(s - m)
            rs.append(1.0 / jnp.sum(e, axis=-1, keepdims=True))
            es.append(e.astype(BF16))
        for p in range(N_PAIRS):
            cols = slice(p * LANES, (p + 1) * LANES)
            o2 = jnp.dot(es[p], vb_h[pl.ds(k0, NA_KEYS), cols], preferred_element_type=F32) * rs[p]
            o_scr[i * GRID_W:(i + 1) * GRID_W, QA_W + p * LANES:QA_W + (p + 1) * LANES] = (
                jnp.where(lo_row, o2[:GRID_W], o2[GRID_W:]))

    def project(j, _):
        rows = slice(j * BLK, (j + 1) * BLK)
        o = o_scr[rows, :]
        na = _rms(o[:, :QA_W], gga_ref[...]).astype(BF16)
        nb = _rms(o[:, QA_W:], ggb_ref[...]).astype(BF16)
        mix = jnp.dot(jnp.concatenate([na, nb], axis=-1), wout_ref[...],
                      preferred_element_type=F32)
        hout_ref[rows, :] = h_ref[rows, :] + _rms(mix, _row(g_ref, N_MIX_POST))

    no_scores = lambda j: None
    units = []
    for j in range(WA_BLOCKS):
        units.append((wa_scores, wa_finish, j))
        for i in range(j * NA_TILE_ROWS // WA_BLOCKS, (j + 1) * NA_TILE_ROWS // WA_BLOCKS):
            units.append((na_scores, na_finish, i))
        units.append((no_scores, project, j))
    pending = [scores(idx) for scores, _, idx in units[:SCORE_LOOKAHEAD]]
    for k, (_, finish, idx) in enumerate(units):
        ahead = k + SCORE_LOOKAHEAD
        if ahead < len(units):
            pending.append(units[ahead][0](units[ahead][2]))
        finish(idx, pending.pop(0))


def _layer_spec(shape, layer):
    nd = len(shape)
    return pl.BlockSpec((None,) + tuple(shape), lambda *_: (layer,) + (0,) * nd,
                        pipeline_mode=pl.Buffered(1))


def _const_spec(shape):
    nd = len(shape)
    return pl.BlockSpec(shape, lambda *_: (0,) * nd, pipeline_mode=pl.Buffered(1))


def _params(semantics):
    return pltpu.CompilerParams(dimension_semantics=semantics, vmem_limit_bytes=VMEM_LIMIT_BYTES)


def _ffn_scratch():
    return [pltpu.VMEM((TOKEN_TILE, D_MODEL), BF16), pltpu.VMEM((TOKEN_TILE, D_MODEL), F32)]


def _ffn_weight_specs(layer):
    return [_layer_spec((D_MODEL, D_FF), layer), _layer_spec((D_MODEL, D_FF), layer),
            _layer_spec((D_FF, D_MODEL), layer)]


def _token_tile(w):
    return pl.BlockSpec((TOKEN_TILE, w), lambda i: (i, 0))


def _next_token_tile(w, n_tiles):
    return pl.BlockSpec((TOKEN_TILE, w), lambda i: (jnp.minimum(i + 1, n_tiles - 1), 0))


def _ffn_qkv(layer, h, g, wg, wu, wd, win):
    n_tok = h.shape[0]
    n_tiles = n_tok // TOKEN_TILE
    widths = (QA_W, KVA_W, KVA_W, QB_W, QB_W, QB_W)
    return pl.pallas_call(
        _ffn_qkv_kernel,
        grid=(n_tiles,),
        in_specs=[_token_tile(D_MODEL), _next_token_tile(D_MODEL, n_tiles),
                  _layer_spec((N_NORMS, D_MODEL), layer)]
                 + _ffn_weight_specs(layer) + [_layer_spec((D_MODEL, IN_WIDTH), layer)],
        out_specs=[_token_tile(D_MODEL)] + [_token_tile(w) for w in widths],
        out_shape=[jax.ShapeDtypeStruct((n_tok, D_MODEL), F32)]
                  + [jax.ShapeDtypeStruct((n_tok, w), BF16) for w in widths],
        scratch_shapes=_ffn_scratch(),
        compiler_params=_params(("arbitrary",)),
        name="ffn1_qkv",
    )(h, h, g, wg, wu, wd, win)


def _ffn_ple(layer, h, p, g, wg, wu, wd, wpg, wpp):
    n_tok = h.shape[0]
    n_tiles = n_tok // TOKEN_TILE
    p_spec = pl.BlockSpec((None, TOKEN_TILE, PLE_DIM), lambda i: (layer, i, 0))
    return pl.pallas_call(
        _ffn_ple_kernel,
        grid=(n_tiles,),
        in_specs=[_token_tile(D_MODEL), _next_token_tile(D_MODEL, n_tiles), p_spec,
                  _layer_spec((N_NORMS, D_MODEL), layer)]
                 + _ffn_weight_specs(layer)
                 + [_layer_spec((D_MODEL, D_MODEL), layer), _layer_spec((PLE_DIM, D_MODEL), layer)],
        out_specs=_token_tile(D_MODEL),
        out_shape=jax.ShapeDtypeStruct((n_tok, D_MODEL), F32),
        scratch_shapes=_ffn_scratch(),
        compiler_params=_params(("arbitrary",)),
        name="ffn2_ple",
    )(h, h, p, g, wg, wu, wd, wpg, wpp)


def _attention(layer, h, qa, ka, va, qb, kb, vb, sink, alibi, nbias, g, gga, ggb, wout):
    n_batch, seq_len, _ = h.shape
    assert seq_len % ATT_TILE == 0 and seq_len // BLK >= 2 and seq_len // GRID_W >= NA_ROWS
    n_wa = seq_len // WA_HALO
    n_na = seq_len // NA_HALO
    wa_per = ATT_TILE // WA_HALO
    na_per = ATT_TILE // NA_HALO

    def cur(w):
        return pl.BlockSpec((None, ATT_TILE, w), lambda b, t: (b, t, 0))

    def halo(rows, w, per, n, side):
        if side < 0:
            return pl.BlockSpec((None, rows, w), lambda b, t: (b, jnp.maximum(t * per - 1, 0), 0))
        return pl.BlockSpec((None, rows, w), lambda b, t: (b, jnp.minimum((t + 1) * per, n - 1), 0))

    def triple(rows, w, per, n):
        return [halo(rows, w, per, n, -1), cur(w), halo(rows, w, per, n, 1)]

    in_specs = ([pl.BlockSpec(memory_space=pltpu.SMEM), cur(D_MODEL), cur(QA_W)]
                + triple(WA_HALO, KVA_W, wa_per, n_wa) + triple(WA_HALO, KVA_W, wa_per, n_wa)
                + [cur(QB_W)]
                + triple(NA_HALO, QB_W, na_per, n_na) + triple(NA_HALO, QB_W, na_per, n_na)
                + [_const_spec(alibi.shape), _layer_spec(nbias.shape[1:], layer),
                   _layer_spec((N_NORMS, D_MODEL), layer),
                   _layer_spec((1, QA_W), layer), _layer_spec((1, QB_W), layer),
                   _layer_spec((D_MODEL, D_MODEL), layer)])
    return pl.pallas_call(
        functools.partial(_attn_kernel, seq_len=seq_len, layer=layer),
        grid=(n_batch, seq_len // ATT_TILE),
        in_specs=in_specs,
        out_specs=cur(D_MODEL),
        out_shape=jax.ShapeDtypeStruct(h.shape, F32),
        scratch_shapes=[pltpu.VMEM((ATT_TILE + 2 * WA_HALO, KVA_W), BF16),
                        pltpu.VMEM((ATT_TILE + 2 * WA_HALO, KVA_W), BF16),
                        pltpu.VMEM((ATT_TILE + 2 * NA_HALO, QB_W), BF16),
                        pltpu.VMEM((ATT_TILE + 2 * NA_HALO, QB_W), BF16),
                        pltpu.VMEM((ATT_TILE, D_MODEL), F32)],
        compiler_params=_params(("parallel", "parallel")),
        name="attention",
    )(sink, h, qa, ka, ka, ka, va, va, va, qb, kb, kb, kb, vb, vb, vb,
      alibi, nbias, g, gga, ggb, wout)


def _alibi_table():
    i = jnp.arange(BLK)[:, None]
    j = jnp.arange(BAND)[None, :]
    rel = j - BLK - i
    dist = jnp.abs(rel).astype(F32)
    slopes = jnp.exp2(-(8.0 / A_HEADS) * jnp.arange(1, A_HEADS + 1, dtype=F32))
    bias = -(slopes[:, None, None] * dist[None])
    in_window = jnp.abs(rel) <= WINDOW
    key_ok = jnp.stack([j >= 0, j >= BLK, j < 2 * BLK])
    ok = in_window[None] & key_ok
    return jnp.where(ok[:, None], bias[None], -jnp.inf)


def _na_bias_tables(rpb):
    c = jnp.arange(GRID_W)
    dc = jnp.clip(c[None, :] - c[:, None], -(NA_COLS - 1), NA_COLS - 1) + (NA_COLS - 1)
    cs = jnp.clip(c - NA_COLS // 2, 0, GRID_W - NA_COLS)
    col_ok = (c[None, :] >= cs[:, None]) & (c[None, :] < cs[:, None] + NA_COLS)
    onehot = (dc[None] == jnp.arange(2 * NA_COLS - 1)[:, None, None]).astype(F32)
    b = jnp.einsum("lhrc,cqk->lhqrk", rpb.astype(F32), onehot, precision=lax.Precision.HIGHEST)
    b = jnp.where(col_ok[:, None, :], b, -jnp.inf)
    b = jnp.stack([b[:, :, :, NA_ROWS - 1 - v:2 * NA_ROWS - 1 - v] for v in range(NA_ROWS)],
                  axis=1)
    return b.reshape(DEPTH, NA_ROWS, N_PAIRS, 2 * GRID_W, NA_KEYS)


def _slot_order(a, axis):
    shape = a.shape
    a = a.reshape(shape[:axis] + (A_KV_HEADS, A_GROUP, HEAD_DIM) + shape[axis + 1:])
    a = jnp.swapaxes(a, axis, axis + 1)
    return a.reshape(shape)


def _prepare_weights(w_ffn1_gate, w_ffn1_up, w_ffn1_down, w_in, w_out, sink, rpb,
                     w_ffn2_gate, w_ffn2_up, w_ffn2_down, w_ple_proj, w_ple_gate, norm_g):
    scale = HEAD_DIM ** -0.5 * LOG2E
    kv_end = QA_W + 2 * KVA_W
    win = jnp.concatenate([_slot_order(w_in[:, :, :QA_W], 2) * scale,
                           w_in[:, :, QA_W:kv_end],
                           w_in[:, :, kv_end:kv_end + QB_W] * scale,
                           w_in[:, :, kv_end + QB_W:]], axis=2).astype(BF16)
    wout = jnp.concatenate([_slot_order(w_out[:, :QA_W], 1), w_out[:, QA_W:]], axis=1).astype(BF16)
    gg = norm_g[:, N_GROUP]
    return dict(
        g=norm_g.astype(F32),
        ffn1=(w_ffn1_gate.astype(BF16), w_ffn1_up.astype(BF16), w_ffn1_down.astype(BF16)),
        ffn2=(w_ffn2_gate.astype(BF16), w_ffn2_up.astype(BF16), w_ffn2_down.astype(BF16)),
        win=win, wout=wout, sink=sink.astype(F32) * LOG2E,
        gga=_slot_order(gg[:, :QA_W], 1)[:, None, :], ggb=gg[:, None, QA_W:],
        nbias=_na_bias_tables(rpb) * LOG2E, alibi=_alibi_table() * LOG2E,
        wpg=w_ple_gate.astype(BF16), wpp=w_ple_proj.astype(BF16))


def _layer(layer, h, p, w):
    n_batch, seq_len, _ = h.shape
    flat = lambda a: a.reshape(n_batch * seq_len, a.shape[-1])
    seq = lambda a: a.reshape(n_batch, seq_len, a.shape[-1])
    h1, qa, ka, va, qb, kb, vb = _ffn_qkv(layer, flat(h), w["g"], *w["ffn1"], w["win"])
    h2 = _attention(layer, seq(h1), seq(qa), seq(ka), seq(va), seq(qb), seq(kb), seq(vb),
                    w["sink"], w["alibi"], w["nbias"], w["g"], w["gga"], w["ggb"], w["wout"])
    h3 = _ffn_ple(layer, flat(h2), p, w["g"], *w["ffn2"], w["wpg"], w["wpp"])
    return seq(h3)


def kernel(x_prompt, x_sample, p_prompt, p_sample, w_ffn1_gate, w_ffn1_up, w_ffn1_down,
           w_in, w_out, sink, rpb, w_ffn2_gate, w_ffn2_up, w_ffn2_down,
           w_ple_proj, w_ple_gate, norm_g):
    w = _prepare_weights(w_ffn1_gate, w_ffn1_up, w_ffn1_down, w_in, w_out, sink, rpb,
                         w_ffn2_gate, w_ffn2_up, w_ffn2_down, w_ple_proj, w_ple_gate, norm_g)
    pp = p_prompt.reshape(DEPTH, -1, PLE_DIM)
    ps = p_sample.reshape(DEPTH, -1, PLE_DIM)
    hp, hs = x_prompt, x_sample
    for layer in range(DEPTH):
        hp = _layer(layer, hp, pp, w)
        hs = _layer(layer, hs, ps, w)
    return (hp, hs)
```
